```python
import math
import jax, jax.numpy as jnp
from jax import lax
import numpy as np

D_MODEL = 1024
BATCH = 16
SEQ = 2048
DEPTH = 2

GRID_W = 64
CTX_LEN = 256
HEAD_DIM = 64
ATTN_Q_HEADS = 8
ATTN_KV_HEADS = 2
ATTN_GROUP = ATTN_Q_HEADS // ATTN_KV_HEADS
WINDOW = 128
ATTN_BLOCK = 128
ROPE_BASE = 10000.0
ATTN_WIDTH = ATTN_Q_HEADS * HEAD_DIM
KV_WIDTH = ATTN_KV_HEADS * HEAD_DIM
POOL_WINDOWS = (2, 4, 8, 16)
POOL_GROUPS = len(POOL_WINDOWS)
POOL_CH = D_MODEL // 8
POOL_WIDTH = POOL_CH * POOL_GROUPS
EVEN_IN = ATTN_WIDTH + 2 * KV_WIDTH + POOL_WIDTH
SPLIT_EVEN = (ATTN_WIDTH, ATTN_WIDTH + KV_WIDTH, ATTN_WIDTH + 2 * KV_WIDTH)
MLSTM_HEADS = 4
MLSTM_WIDTH = D_MODEL
MLSTM_HEAD_DIM = MLSTM_WIDTH // MLSTM_HEADS
MLSTM_CHUNK = 64
SHORT_CONV = 3
ODD_IN = 3 * MLSTM_WIDTH + 4 * MLSTM_HEADS
SPLIT_ODD = (MLSTM_WIDTH, 2 * MLSTM_WIDTH, 3 * MLSTM_WIDTH)
D_FF = 2816
FFN_CONV = 3
EPS = 1e-6
N_EVEN = (DEPTH + 1) // 2
N_ODD = DEPTH // 2

kernel_name = "hybrid_swa_pool_mlstm_dit_block"

F32 = jnp.float32


def rmsnorm(x, g):
    x32 = x.astype(F32)
    y = x32 * lax.rsqrt(jnp.mean(x32 * x32, axis=-1, keepdims=True) + EPS)
    return (y * g.astype(F32)).astype(x.dtype)


def adaln(cvec, w, b):
    m = jax.nn.silu(cvec) @ w + b
    return jnp.split(m[:, None, :], 6, axis=-1)


def modulate(h, shift, scale):
    return h * (1 + scale) + shift


def dwconv1d(x, w, b):
    k = w.shape[0]
    y = lax.conv_general_dilated(x, w[:, None, :].astype(x.dtype), window_strides=(1,),
                                 padding=[(k // 2, k // 2)],
                                 dimension_numbers=('NWC', 'WIO', 'NWC'),
                                 feature_group_count=x.shape[-1])
    return y + b


def axial_rope_tables(t_len, dtype):
    rows = t_len // GRID_W
    row = jnp.repeat(jnp.arange(rows, dtype=F32), GRID_W)
    col = jnp.tile(jnp.arange(GRID_W, dtype=F32), rows)
    n_freq = HEAD_DIM // 4
    inv = ROPE_BASE ** (-jnp.arange(n_freq, dtype=F32) / n_freq)
    ang = jnp.stack([row[:, None] * inv, col[:, None] * inv], axis=1)
    return jnp.cos(ang).astype(dtype), jnp.sin(ang).astype(dtype)


def apply_axial_rope(x, cos, sin):
    b, t, h, d = x.shape
    xs = x.reshape(b, t, h, 2, 2, d // 4)
    x1, x2 = xs[..., 0, :], xs[..., 1, :]
    c, s = cos[None, :, None], sin[None, :, None]
    return jnp.stack([x1 * c - x2 * s, x1 * s + x2 * c], axis=-2).reshape(b, t, h, d)


def windowed_attention(q, k, v, k_ctx, v_ctx, sink):
    b, t = q.shape[:2]
    nb = t // ATTN_BLOCK
    scale = HEAD_DIM ** -0.5
    qb = q.reshape(b, nb, ATTN_BLOCK, ATTN_KV_HEADS, ATTN_GROUP, HEAD_DIM)

    def band(a):
        ab = a.reshape(b, nb, ATTN_BLOCK, ATTN_KV_HEADS, HEAD_DIM)
        ap = jnp.pad(ab, ((0, 0), (1, 1), (0, 0), (0, 0), (0, 0)))
        return jnp.concatenate([ap[:, :-2], ap[:, 1:-1], ap[:, 2:]], axis=2)

    kw, vw = band(k), band(v)
    blk = jnp.arange(nb)[:, None]
    qpos = blk * ATTN_BLOCK + jnp.arange(ATTN_BLOCK)[None, :]
    kpos = (blk - 1) * ATTN_BLOCK + jnp.arange(3 * ATTN_BLOCK)[None, :]
    valid = ((jnp.abs(qpos[:, :, None] - kpos[:, None, :]) <= WINDOW)
             & (kpos[:, None, :] >= 0) & (kpos[:, None, :] < t))
    s_loc = jnp.einsum('bnqhgd,bnkhd->bnhgqk', qb, kw).astype(F32) * scale
    s_loc = jnp.where(valid[None, :, None, None], s_loc, -jnp.inf)
    s_ctx = jnp.einsum('bnqhgd,bchd->bnhgqc', qb, k_ctx).astype(F32) * scale
    sink_l = sink.astype(F32).reshape(1, 1, ATTN_KV_HEADS, ATTN_GROUP, 1)
    m = jnp.maximum(jnp.maximum(s_loc.max(-1), s_ctx.max(-1)), sink_l)
    p_loc = jnp.exp(s_loc - m[..., None])
    p_ctx = jnp.exp(s_ctx - m[..., None])
    denom = p_loc.sum(-1) + p_ctx.sum(-1) + jnp.exp(sink_l - m)
    o = (jnp.einsum('bnhgqk,bnkhd->bnqhgd', p_loc.astype(v.dtype), vw)
         + jnp.einsum('bnhgqc,bchd->bnqhgd', p_ctx.astype(v.dtype), v_ctx))
    o = o / jnp.moveaxis(denom, -1, 2)[..., None]
    return o.reshape(b, t, ATTN_WIDTH).astype(q.dtype)


def context_attention(q, k, v, sink):
    b, c = q.shape[:2]
    qg = q.reshape(b, c, ATTN_KV_HEADS, ATTN_GROUP, HEAD_DIM)
    s = jnp.einsum('bqhgd,bkhd->bhgqk', qg, k).astype(F32) * HEAD_DIM ** -0.5
    sk = jnp.broadcast_to(sink.astype(F32).reshape(ATTN_KV_HEADS, ATTN_GROUP, 1, 1), s.shape[:-1] + (1,))
    p = jax.nn.softmax(jnp.concatenate([s, sk], axis=-1), axis=-1)[..., :-1]
    o = jnp.einsum('bhgqk,bkhd->bqhgd', p.astype(v.dtype), v)
    return o.reshape(b, c, ATTN_WIDTH)


def pool_mixer(u, pool_w, pool_scale):
    b, t, _ = u.shape
    u32 = u.astype(F32)
    pre = jnp.pad(jnp.cumsum(u32, axis=1), ((0, 0), (1, 0), (0, 0)))
    pos = jnp.arange(t)
    means = []
    for gi, w in enumerate(POOL_WINDOWS):
        lo = jnp.clip(pos - w // 2, 0, t)
        hi = jnp.clip(pos - w // 2 + w, 0, t)
        pg = pre[..., gi * POOL_CH:(gi + 1) * POOL_CH]
        means.append((pg[:, hi] - pg[:, lo]) / (hi - lo).astype(F32)[:, None])
    d = (jnp.concatenate(means, axis=-1) - u32).astype(u.dtype).reshape(b, t, POOL_GROUPS, POOL_CH)
    y = jnp.einsum('btgc,gce->btge', d, pool_w).reshape(b, t, POOL_WIDTH)
    return y * pool_scale


def even_mixer(h, hc, w_in, sink, pool_w, pool_scale, w_out, need_ctx):
    b, t, _ = h.shape
    cos, sin = axial_rope_tables(t, h.dtype)
    q, k, v, u = jnp.split(h @ w_in, SPLIT_EVEN, axis=-1)
    q = apply_axial_rope(q.reshape(b, t, ATTN_Q_HEADS, HEAD_DIM), cos, sin)
    k = apply_axial_rope(k.reshape(b, t, ATTN_KV_HEADS, HEAD_DIM), cos, sin)
    v = v.reshape(b, t, ATTN_KV_HEADS, HEAD_DIM)
    c = hc.shape[1]
    if need_ctx:
        qc, kc, vc, uc = jnp.split(hc @ w_in, SPLIT_EVEN, axis=-1)
    else:
        kc, vc = jnp.split(hc @ w_in[:, ATTN_WIDTH:ATTN_WIDTH + 2 * KV_WIDTH], 2, axis=-1)
    kc = kc.reshape(b, c, ATTN_KV_HEADS, HEAD_DIM)
    vc = vc.reshape(b, c, ATTN_KV_HEADS, HEAD_DIM)
    attn = windowed_attention(q, k, v, kc, vc, sink)
    out = jnp.concatenate([attn, pool_mixer(u, pool_w, pool_scale)], axis=-1) @ w_out
    out_c = None
    if need_ctx:
        attn_c = context_attention(qc.reshape(b, c, ATTN_Q_HEADS, HEAD_DIM), kc, vc, sink)
        out_c = jnp.concatenate([attn_c, pool_mixer(uc, pool_w, pool_scale)], axis=-1) @ w_out
    return out, out_c


def mlstm_inputs(h, w_in, gate_b, conv_w, conv_b, q_w, k_w, need_out):
    b, t, _ = h.shape
    if need_out:
        u, v, og, gates = jnp.split(h @ w_in, SPLIT_ODD, axis=-1)
    else:
        w_sel = jnp.concatenate([w_in[:, :2 * MLSTM_WIDTH], w_in[:, 3 * MLSTM_WIDTH:]], axis=1)
        u, v, gates = jnp.split(h @ w_sel, (MLSTM_WIDTH, 2 * MLSTM_WIDTH), axis=-1)
        og = None
    uc = jax.nn.silu(dwconv1d(u, conv_w, conv_b))
    uh = uc.reshape(b, t, MLSTM_HEADS, MLSTM_HEAD_DIM)
    k = jnp.einsum('bthd,hde->bthe', uh, k_w) * MLSTM_HEAD_DIM ** -0.5
    q = jnp.einsum('bthd,hde->bthe', uh, q_w) if need_out else None
    v = v.reshape(b, t, MLSTM_HEADS, MLSTM_HEAD_DIM)
    g = gates.astype(F32).reshape(b, t, 4, MLSTM_HEADS) + gate_b.astype(F32)
    g_fwd = (g[:, :, 0], jax.nn.log_sigmoid(g[:, :, 1]))
    g_bwd = (g[:, :, 2], jax.nn.log_sigmoid(g[:, :, 3]))
    return uc, og, q, k, v, g_fwd, g_bwd


def mlstm_zero_state(b):
    return (jnp.zeros((b, MLSTM_HEADS, MLSTM_HEAD_DIM, MLSTM_HEAD_DIM), F32),
            jnp.zeros((b, MLSTM_HEADS, MLSTM_HEAD_DIM), F32),
            jnp.zeros((b, MLSTM_HEADS), F32))


def mlstm_final_state(k, v, ig, lf):
    k, v = k.astype(F32), v.astype(F32)
    bsum = jnp.cumsum(lf, axis=1)
    b_end = bsum[:, -1]
    w = b_end[:, None] - bsum + ig
    m = jnp.maximum(b_end, w.max(axis=1))
    a = jnp.exp(w - m[:, None])
    cmat = jnp.einsum('bth,bthv,bthk->bhvk', a, v, k)
    nvec = jnp.einsum('bth,bthk->bhk', a, k)
    return (cmat, nvec, m)


def mlstm_chunkwise(q, k, v, ig, lf, state):
    b, t, h, d = q.shape
    nc, ln = t // MLSTM_CHUNK, MLSTM_CHUNK

    def to_chunks(a):
        a = a.astype(F32).reshape((b, nc, ln, h) + a.shape[3:])
        return jnp.moveaxis(a, (1, 3), (0, 2))

    causal = jnp.tril(jnp.ones((ln, ln), bool))

    def step(carry, xs):
        cmat, nvec, m = carry
        qc, kc, vc, ic, fc = xs
        bc = jnp.cumsum(fc, axis=-1)
        dlog = jnp.where(causal, bc[..., :, None] - bc[..., None, :] + ic[..., None, :], -jnp.inf)
        inter = bc + m[..., None]
        m_t = jnp.maximum(inter, dlog.max(-1))
        dw = jnp.exp(dlog - m_t[..., None])
        iw = jnp.exp(inter - m_t)
        s = jnp.einsum('bhtd,bhsd->bhts', qc, kc) * dw
        num = jnp.einsum('bhts,bhsd->bhtd', s, vc) + iw[..., None] * jnp.einsum('bhvk,bhtk->bhtv', cmat, qc)
        den = s.sum(-1) + iw * jnp.einsum('bhk,bhtk->bht', nvec, qc)
        hout = num / jnp.maximum(jnp.abs(den), jnp.exp(-m_t))[..., None]
        m_new = m_t[..., -1]
        wts = jnp.exp(bc[..., -1:] - bc + ic - m_new[..., None])
        decay = jnp.exp(bc[..., -1] + m - m_new)
        cmat = decay[..., None, None] * cmat + jnp.einsum('bhs,bhsv,bhsk->bhvk', wts, vc, kc)
        nvec = decay[..., None] * nvec + jnp.einsum('bhs,bhsk->bhk', wts, kc)
        return (cmat, nvec, m_new), hout

    _, hs = lax.scan(step, state, (to_chunks(q), to_chunks(k), to_chunks(v), to_chunks(ig), to_chunks(lf)))
    return jnp.moveaxis(hs, (0, 2), (1, 3)).reshape(b, t, h, d)


def mlstm_output(uc, og, q, k, v, g_fwd, g_bwd, state_f, state_b, norm_g, skip, w_out):
    b, t = uc.shape[:2]
    flip = lambda a: jnp.flip(a, axis=1)
    h_f = mlstm_chunkwise(q, k, v, g_fwd[0], g_fwd[1], state_f)
    h_b = flip(mlstm_chunkwise(flip(q), flip(k), flip(v), flip(g_bwd[0]), flip(g_bwd[1]), state_b))
    hs = h_f + h_b
    hs = hs * lax.rsqrt(jnp.mean(hs * hs, axis=-1, keepdims=True) + EPS)
    hs = (hs.reshape(b, t, MLSTM_WIDTH) * norm_g.astype(F32)).astype(uc.dtype)
    y = jax.nn.sigmoid(og) * (hs + skip * uc)
    return y @ w_out


def odd_mixer(h, hc, w_in, gate_b, conv_w, conv_b, q_w, k_w, norm_g, skip, w_out, need_ctx):
    flip = lambda a: jnp.flip(a, axis=1)
    uc, og, q, k, v, gf, gb = mlstm_inputs(h, w_in, gate_b, conv_w, conv_b, q_w, k_w, True)
    uc_c, og_c, q_c, k_c, v_c, gf_c, gb_c = mlstm_inputs(hc, w_in, gate_b, conv_w, conv_b, q_w, k_w, need_ctx)
    state_f = mlstm_final_state(k_c, v_c, gf_c[0], gf_c[1])
    state_b = mlstm_final_state(flip(k_c), flip(v_c), flip(gb_c[0]), flip(gb_c[1]))
    out = mlstm_output(uc, og, q, k, v, gf, gb, state_f, state_b, norm_g, skip, w_out)
    out_c = None
    if need_ctx:
        zero = mlstm_zero_state(hc.shape[0])
        out_c = mlstm_output(uc_c, og_c, q_c, k_c, v_c, gf_c, gb_c, zero, zero, norm_g, skip, w_out)
    return out, out_c


def ffn_sublayer(s, shift, scale, gate, g_pre, g_post, w_up, conv_w, conv_b, w_down):
    h = modulate(rmsnorm(s, g_pre), shift, scale)
    a = dwconv1d(h @ w_up, conv_w, conv_b)
    a_gate, a_val = jnp.split(a, 2, axis=-1)
    return s + gate * rmsnorm((jax.nn.silu(a_gate) * a_val) @ w_down, g_post)


def setup_inputs(seed: int = 0) -> dict:
    key = jax.random.key(seed)
    ks = jax.random.split(key, 24)
    nrm = lambda k, shape, s: jax.random.normal(k, shape, F32) * s
    d = D_MODEL
    fb = jnp.linspace(3.0, 6.0, MLSTM_HEADS)
    zh = jnp.zeros((MLSTM_HEADS,), F32)
    gate_base = jnp.stack([zh, fb, zh, fb])
    return {
        "x": nrm(ks[0], (BATCH, SEQ, d), 1.0),
        "c": nrm(ks[1], (BATCH, d), 1.0),
        "ctx": nrm(ks[2], (BATCH, CTX_LEN, d), 1.0),
        "c_ctx": nrm(ks[3], (d,), 1.0),
        "mod_w": nrm(ks[4], (DEPTH, d, 6 * d), d ** -0.5),
        "mod_b": nrm(ks[5], (DEPTH, 6 * d), 0.01),
        "norm_g": 1.0 + nrm(ks[6], (DEPTH, 4, d), 0.05),
        "attn_in_w": nrm(ks[7], (N_EVEN, d, EVEN_IN), d ** -0.5),
        "attn_sink": nrm(ks[8], (N_EVEN, ATTN_Q_HEADS), 0.5),
        "pool_w": nrm(ks[9], (N_EVEN, POOL_GROUPS, POOL_CH, POOL_CH), POOL_CH ** -0.5),
        "pool_scale": 1.0 + nrm(ks[10], (N_EVEN, POOL_WIDTH), 0.1),
        "attn_out_w": nrm(ks[11], (N_EVEN, ATTN_WIDTH + POOL_WIDTH, d), (ATTN_WIDTH + POOL_WIDTH) ** -0.5),
        "rec_in_w": nrm(ks[12], (N_ODD, d, ODD_IN), d ** -0.5),
        "rec_gate_b": gate_base[None] + nrm(ks[13], (N_ODD, 4, MLSTM_HEADS), 0.1),
        "rec_conv_w": nrm(ks[14], (N_ODD, SHORT_CONV, MLSTM_WIDTH), SHORT_CONV ** -0.5),
        "rec_conv_b": nrm(ks[15], (N_ODD, MLSTM_WIDTH), 0.01),
        "rec_q_w": nrm(ks[16], (N_ODD, MLSTM_HEADS, MLSTM_HEAD_DIM, MLSTM_HEAD_DIM), MLSTM_HEAD_DIM ** -0.5),
        "rec_k_w": nrm(ks[17], (N_ODD, MLSTM_HEADS, MLSTM_HEAD_DIM, MLSTM_HEAD_DIM), MLSTM_HEAD_DIM ** -0.5),
        "rec_norm_g": 1.0 + nrm(ks[18], (N_ODD, MLSTM_WIDTH), 0.05),
        "rec_skip": 1.0 + nrm(ks[19], (N_ODD, MLSTM_WIDTH), 0.1),
        "rec_out_w": nrm(ks[20], (N_ODD, MLSTM_WIDTH, d), MLSTM_WIDTH ** -0.5),
        "ffn_up_w": nrm(ks[21], (DEPTH, d, 2 * D_FF), d ** -0.5),
        "ffn_conv_w": nrm(ks[22], (DEPTH, FFN_CONV, 2 * D_FF), FFN_CONV ** -0.5),
        "ffn_conv_b": nrm(ks[23], (DEPTH, 2 * D_FF), 0.01),
        "ffn_down_w": nrm(jax.random.fold_in(ks[23], 1), (DEPTH, D_FF, d), D_FF ** -0.5),
    }


def reference(x, c, ctx, c_ctx, mod_w, mod_b, norm_g, attn_in_w, attn_sink, pool_w, pool_scale, attn_out_w,
              rec_in_w, rec_gate_b, rec_conv_w, rec_conv_b, rec_q_w, rec_k_w, rec_norm_g, rec_skip, rec_out_w,
              ffn_up_w, ffn_conv_w, ffn_conv_b, ffn_down_w):
    s_ctx = ctx
    for l in range(DEPTH):
        last = l == DEPTH - 1
        i = l // 2
        sh1, sc1, g1, sh2, sc2, g2 = adaln(c, mod_w[l], mod_b[l])
        csh1, csc1, cg1, csh2, csc2, cg2 = adaln(c_ctx[None, :], mod_w[l], mod_b[l])
        h = modulate(rmsnorm(x, norm_g[l, 0]), sh1, sc1)
        hc = modulate(rmsnorm(s_ctx, norm_g[l, 0]), csh1, csc1)
        if l % 2 == 0:
            mix, mix_c = even_mixer(h, hc, attn_in_w[i], attn_sink[i], pool_w[i], pool_scale[i],
                                    attn_out_w[i], not last)
        else:
            mix, mix_c = odd_mixer(h, hc, rec_in_w[i], rec_gate_b[i], rec_conv_w[i], rec_conv_b[i],
                                   rec_q_w[i], rec_k_w[i], rec_norm_g[i], rec_skip[i], rec_out_w[i], not last)
        x = x + g1 * rmsnorm(mix, norm_g[l, 1])
        x = ffn_sublayer(x, sh2, sc2, g2, norm_g[l, 2], norm_g[l, 3],
                         ffn_up_w[l], ffn_conv_w[l], ffn_conv_b[l], ffn_down_w[l])
        if not last:
            s_ctx = s_ctx + cg1 * rmsnorm(mix_c, norm_g[l, 1])
            s_ctx = ffn_sublayer(s_ctx, csh2, csc2, cg2, norm_g[l, 2], norm_g[l, 3],
                                 ffn_up_w[l], ffn_conv_w[l], ffn_conv_b[l], ffn_down_w[l])
    return x
```

```python
import functools

import jax
import jax.numpy as jnp
from jax import lax
from jax.experimental import pallas as pl
from jax.experimental.pallas import tpu as pltpu

F32 = jnp.float32
BF16 = jnp.bfloat16

GRID_W = 64
HEAD_DIM = 64
ATTN_Q_HEADS = 8
ATTN_KV_HEADS = 2
ATTN_BLOCK = 128
ROPE_BASE = 10000.0
POOL_WINDOWS = (2, 4, 8, 16)
POOL_CH = 128
MLSTM_HEADS = 4
MLSTM_HEAD_DIM = 256
MLSTM_CHUNK = 256
EPS = 1e-6
NEG = -1e30

LANES = 128
SUBLANES = 8
VMEM_LIMIT = 56 * 1024 * 1024


def _params(*sem):
    return pltpu.CompilerParams(dimension_semantics=sem, vmem_limit_bytes=VMEM_LIMIT)


def _rms(x, g):
    return x * lax.rsqrt(jnp.mean(x * x, axis=-1, keepdims=True) + EPS) * g


def _sigmoid(x):
    return 1.0 / (1.0 + jnp.exp(-x))


def _dot(a, b):
    return jnp.dot(a, b, preferred_element_type=F32)


def _dot_nt(a, b):
    return lax.dot_general(a, b, (((1,), (1,)), ((), ())), preferred_element_type=F32)


def _dot_tn(a, b):
    return lax.dot_general(a, b, (((0,), (0,)), ((), ())), preferred_element_type=F32)


def _const_spec(shape):
    nd = len(shape)
    return pl.BlockSpec(shape, lambda *_: (0,) * nd)


def _adaln_kernel(c_ref, w_ref, b_ref, o_ref):
    c = c_ref[...]
    s = c * _sigmoid(c)
    o_ref[...] = _dot(s.astype(BF16), w_ref[...].astype(BF16)) + b_ref[...]


def _adaln(c_all, mod_w, mod_b):
    depth, d, n6 = mod_w.shape
    rows = c_all.shape[0]
    tn = 1536
    return pl.pallas_call(
        _adaln_kernel,
        out_shape=jax.ShapeDtypeStruct((depth, rows, n6), F32),
        grid=(depth, n6 // tn),
        in_specs=[
            pl.BlockSpec((rows, d), lambda l, j: (0, 0)),
            pl.BlockSpec((None, d, tn), lambda l, j: (l, 0, j)),
            pl.BlockSpec((None, 1, tn), lambda l, j: (l, 0, j)),
        ],
        out_specs=pl.BlockSpec((None, rows, tn), lambda l, j: (l, 0, j)),
        compiler_params=_params("arbitrary", "arbitrary"),
        name="adaln",
    )(c_all, mod_w, mod_b.reshape(depth, 1, n6))


def _in0_kernel(*refs, rope):
    if rope:
        x_ref, mod_ref, g_ref, w_ref, cq_ref, sq_ref, ck_ref, sk_ref, q_ref, k_ref, v_ref, u_ref = refs
    else:
        x_ref, mod_ref, g_ref, w_ref, q_ref, k_ref, v_ref, u_ref = refs
    x = x_ref[...]
    h = _rms(x, g_ref[...]) * (1.0 + mod_ref[1:2, :]) + mod_ref[0:1, :]
    y = _dot(h.astype(BF16), w_ref[...])
    nq = q_ref.shape[1]
    nk = k_ref.shape[1]
    if rope:
        tm = x.shape[0]
        lane = lax.broadcasted_iota(jnp.int32, (tm, LANES), 1)
        even = (lane // 16) % 2 == 0

        def rot(a, c, s):
            sw = jnp.where(even, pltpu.roll(a, LANES - 16, 1), pltpu.roll(a, 16, 1))
            return a * c + sw * s

        cq, sq, ck, sk = cq_ref[...], sq_ref[...], ck_ref[...], sk_ref[...]
        for j in range(nq // LANES):
            q_ref[:, j * LANES:(j + 1) * LANES] = rot(y[:, j * LANES:(j + 1) * LANES], cq, sq).astype(BF16)
        for j in range(nk // LANES):
            o = nq + j * LANES
            k_ref[:, j * LANES:(j + 1) * LANES] = rot(y[:, o:o + LANES], ck, sk).astype(BF16)
    else:
        q_ref[...] = (y[:, :nq] * (HEAD_DIM ** -0.5)).astype(BF16)
        k_ref[...] = y[:, nq:nq + nk].astype(BF16)
    v_ref[...] = y[:, nq + nk:nq + 2 * nk].astype(BF16)
    u_ref[...] = y[:, nq + 2 * nk:].astype(BF16)


def _in0(x, mod, mod_row, g, w, tables):
    bsz, t, d = x.shape
    tm = min(t, 512)
    rope = tables is not None
    nq, nk, nu = 512, 256, 512
    row = (lambda b, i: (b, 0, 0)) if mod_row is None else (lambda b, i: (mod_row, 0, 0))
    in_specs = [
        pl.BlockSpec((None, tm, d), lambda b, i: (b, i, 0)),
        pl.BlockSpec((None, 6, d), row),
        _const_spec((1, d)),
        _const_spec(w.shape),
    ]
    args = [x, mod, g, w]
    if rope:
        in_specs += [pl.BlockSpec((tm, LANES), lambda b, i: (i, 0))] * 4
        args += list(tables)
    outs = [(nq, BF16), (nk, BF16), (nk, BF16), (nu, BF16)]
    return pl.pallas_call(
        functools.partial(_in0_kernel, rope=rope),
        out_shape=[jax.ShapeDtypeStruct((bsz, t, n), dt) for n, dt in outs],
        grid=(bsz, t // tm),
        in_specs=in_specs,
        out_specs=[pl.BlockSpec((None, tm, n), lambda b, i: (b, i, 0)) for n, _ in outs],
        compiler_params=_params("parallel", "parallel"),
        name="in0_rope" if rope else "in0_ctx",
    )(*args)


def _mix0_kernel(*refs, band, t_len, tq):
    if band:
        (sink_ref, x_ref, q_ref, k_ref, v_ref, kc_ref, vc_ref, u_ref, pw_ref, ps_ref, wo_ref, g_ref, mod_ref,
         o_ref, cat_ref) = refs
    else:
        (sink_ref, x_ref, q_ref, kc_ref, vc_ref, u_ref, pw_ref, ps_ref, wo_ref, g_ref, mod_ref,
         o_ref, cat_ref) = refs
    blk = ATTN_BLOCK
    nb = t_len // blk
    nsub = tq // blk
    tstep = pl.program_id(1)
    n_ctx = kc_ref.shape[0]
    n_keys = (3 * blk if band else 0) + n_ctx

    qi = lax.broadcasted_iota(jnp.int32, (blk, n_keys), 0)
    kj = lax.broadcasted_iota(jnp.int32, (blk, n_keys), 1)
    if band:
        in_band = (kj >= qi) & (kj <= qi + 2 * blk)
        is_ctx = kj >= 3 * blk
    lane_k = lax.broadcasted_iota(jnp.int32, (n_keys, LANES), 1) < HEAD_DIM
    lane_q = lax.broadcasted_iota(jnp.int32, (blk, LANES), 1) < HEAD_DIM
    row3 = lax.broadcasted_iota(jnp.int32, (blk, 3 * blk), 0)
    col3 = lax.broadcasted_iota(jnp.int32, (blk, 3 * blk), 1) - blk
    pool_band = [((col3 >= row3 - w // 2) & (col3 <= row3 + w // 2 - 1)).astype(BF16) for w in POOL_WINDOWS]
    row_pos = lax.broadcasted_iota(jnp.int32, (blk, LANES), 0)

    for j in range(nsub):
        n = tstep * nsub + j
        r0 = pl.multiple_of(n * blk, blk)
        ps = pl.multiple_of(jnp.maximum(n - 1, 0) * blk, blk)
        ns = pl.multiple_of(jnp.minimum(n + 1, nb - 1) * blk, blk)
        has_prev = n > 0
        has_next = n < nb - 1
        rows = slice(j * blk, (j + 1) * blk)

        if band:
            kb = jnp.concatenate([k_ref[pl.ds(ps, blk), :], k_ref[pl.ds(r0, blk), :], k_ref[pl.ds(ns, blk), :],
                                  kc_ref[...]], axis=0)
            vb = jnp.concatenate([v_ref[pl.ds(ps, blk), :], v_ref[pl.ds(r0, blk), :], v_ref[pl.ds(ns, blk), :],
                                  vc_ref[...]], axis=0)
            valid = is_ctx | (in_band & ((kj >= blk) | has_prev) & ((kj < 2 * blk) | has_next))
        else:
            kb = kc_ref[...]
            vb = vc_ref[...]
            valid = None
        zero = jnp.zeros((), BF16)
        for hk in range(ATTN_KV_HEADS):
            kk = kb[:, hk * LANES:(hk + 1) * LANES]
            vv = vb[:, hk * LANES:(hk + 1) * LANES]
            k_lo = jnp.where(lane_k, kk, zero)
            k_hi = jnp.where(lane_k, zero, kk)
            v_cat = jnp.concatenate([jnp.where(lane_k, vv, zero), jnp.where(lane_k, zero, vv)], axis=0)
            for pair in range(2):
                c = hk * 2 + pair
                qc = q_ref[rows, c * LANES:(c + 1) * LANES]
                probs, inv = [], []
                for half, kh in enumerate((k_lo, k_hi)):
                    s = _dot_nt(qc, kh)
                    if valid is not None:
                        s = jnp.where(valid, s, NEG)
                    snk = sink_ref[2 * c + half]
                    m = jnp.maximum(jnp.max(s, axis=-1, keepdims=True), snk)
                    p = jnp.exp(s - m)
                    den = jnp.sum(p, axis=-1, keepdims=True) + jnp.exp(snk - m)
                    probs.append(p.astype(BF16))
                    inv.append(1.0 / den)
                o = _dot(jnp.concatenate(probs, axis=1), v_cat)
                o = o * jnp.where(lane_q, inv[0], inv[1])
                cat_ref[rows, c * LANES:(c + 1) * LANES] = o.astype(BF16)

        uo = u_ref[pl.ds(r0, blk), :]
        up = jnp.where(has_prev, u_ref[pl.ds(ps, blk), :], zero)
        un = jnp.where(has_next, u_ref[pl.ds(ns, blk), :], zero)
        ub = jnp.concatenate([up, uo, un], axis=0)
        pos = r0 + row_pos
        for gi, w in enumerate(POOL_WINDOWS):
            cs = slice(gi * POOL_CH, (gi + 1) * POOL_CH)
            sums = _dot(pool_band[gi], ub[:, cs])
            cnt = jnp.minimum(pos + w // 2, t_len) - jnp.maximum(pos - w // 2, 0)
            dlt = sums / cnt.astype(F32) - uo[:, cs].astype(F32)
            y = _dot(dlt.astype(BF16), pw_ref[gi]) * ps_ref[:, cs]
            off = ATTN_Q_HEADS * HEAD_DIM + gi * POOL_CH
            cat_ref[rows, off:off + POOL_CH] = y.astype(BF16)

    y = _dot(cat_ref[...], wo_ref[...])
    o_ref[...] = x_ref[...] + mod_ref[2:3, :] * _rms(y, g_ref[...])


def _mix0(x, q, k4, v4, kc4, vc4, u, sink, pool_w, pool_scale, w_out, g, mod, mod_row):
    bsz, t, d = x.shape
    band = k4 is not None
    tq = min(t, 512)
    n_ctx = kc4.shape[1]
    row = (lambda b, i: (b, 0, 0)) if mod_row is None else (lambda b, i: (mod_row, 0, 0))
    full = lambda n, w: pl.BlockSpec((None, n, w), lambda b, i: (b, 0, 0))
    in_specs = [
        pl.BlockSpec(memory_space=pltpu.SMEM),
        pl.BlockSpec((None, tq, d), lambda b, i: (b, i, 0)),
        pl.BlockSpec((None, tq, q.shape[2]), lambda b, i: (b, i, 0)),
    ]
    args = [sink, x, q]
    if band:
        in_specs += [full(t, k4.shape[2]), full(t, v4.shape[2])]
        args += [k4, v4]
    in_specs += [full(n_ctx, kc4.shape[2]), full(n_ctx, vc4.shape[2]), full(t, u.shape[2]),
                 _const_spec(pool_w.shape), _const_spec(pool_scale.shape), _const_spec(w_out.shape),
                 _const_spec(g.shape), pl.BlockSpec((None, 6, d), row)]
    args += [kc4, vc4, u, pool_w, pool_scale, w_out, g, mod]
    return pl.pallas_call(
        functools.partial(_mix0_kernel, band=band, t_len=t, tq=tq),
        out_shape=jax.ShapeDtypeStruct((bsz, t, d), F32),
        grid=(bsz, t // tq),
        in_specs=in_specs,
        out_specs=pl.BlockSpec((None, tq, d), lambda b, i: (b, i, 0)),
        scratch_shapes=[pltpu.VMEM((tq, w_out.shape[0]), BF16)],
        compiler_params=_params("parallel", "parallel"),
        name="mix0_band" if band else "mix0_ctx",
    )(*args)


HALO = SUBLANES


def _halo_specs(tm, t, d):
    nblk = tm // HALO
    last = t // HALO - 1
    return [
        pl.BlockSpec((None, tm, d), lambda b, i: (b, i, 0)),
        pl.BlockSpec((None, HALO, d), lambda b, i: (b, jnp.maximum(i * nblk - 1, 0), 0)),
        pl.BlockSpec((None, HALO, d), lambda b, i: (b, jnp.minimum((i + 1) * nblk, last), 0)),
    ]


def _conv3(u, cw, cb, tm):
    return (cw[0:1, :] * u[HALO - 1:HALO - 1 + tm, :] + cw[1:2, :] * u[HALO:HALO + tm, :]
            + cw[2:3, :] * u[HALO + 1:HALO + 1 + tm, :] + cb)


def _ffn_kernel(xm_ref, xp_ref, xn_ref, mod_ref, gpre_ref, gpost_ref, wup_ref, cw_ref, cb_ref, wdn_ref,
                o_ref, hb_ref, acc_ref, *, tm, nt):
    i = pl.program_id(1)
    shift, scale, gate = mod_ref[3:4, :], mod_ref[4:5, :], mod_ref[5:6, :]
    gpre = gpre_ref[...]

    def pre(xv):
        return _rms(xv, gpre) * (1.0 + scale) + shift

    xm = xm_ref[...]
    hp = jnp.where(i > 0, pre(xp_ref[...]), 0.0)
    hn = jnp.where(i < nt - 1, pre(xn_ref[...]), 0.0)
    hb_ref[...] = jnp.concatenate([hp, pre(xm), hn], axis=0).astype(BF16)
    acc_ref[...] = jnp.zeros_like(acc_ref)
    nchunk = wdn_ref.shape[0]

    def body(c, carry):
        hb = hb_ref[...]
        ag = _conv3(_dot(hb, wup_ref[c]), cw_ref[c], cb_ref[c], tm)
        av = _conv3(_dot(hb, wup_ref[c + nchunk]), cw_ref[c + nchunk], cb_ref[c + nchunk], tm)
        act = ag * _sigmoid(ag) * av
        acc_ref[...] += _dot(act.astype(BF16), wdn_ref[c])
        return carry

    lax.fori_loop(0, nchunk, body, 0)
    o_ref[...] = xm + gate * _rms(acc_ref[...], gpost_ref[...])


def _ffn(x, mod, mod_row, gpre, gpost, wup3, cw3, cb3, wdn3):
    bsz, t, d = x.shape
    tm = min(t, 512)
    nt = t // tm
    row = (lambda b, i: (b, 0, 0)) if mod_row is None else (lambda b, i: (mod_row, 0, 0))
    in_specs = _halo_specs(tm, t, d) + [
        pl.BlockSpec((None, 6, d), row), _const_spec(gpre.shape), _const_spec(gpost.shape),
        _const_spec(wup3.shape), _const_spec(cw3.shape), _const_spec(cb3.shape), _const_spec(wdn3.shape)]
    return pl.pallas_call(
        functools.partial(_ffn_kernel, tm=tm, nt=nt),
        out_shape=jax.ShapeDtypeStruct((bsz, t, d), F32),
        grid=(bsz, nt),
        in_specs=in_specs,
        out_specs=pl.BlockSpec((None, tm, d), lambda b, i: (b, i, 0)),
        scratch_shapes=[pltpu.VMEM((tm + 2 * HALO, d), BF16), pltpu.VMEM((tm, d), F32)],
        compiler_params=_params("parallel", "parallel"),
        name="ffn",
    )(x, x, x, mod, gpre, gpost, wup3, cw3, cb3, wdn3)


def _in1_kernel(*refs, tm, nt, full):
    if full:
        (xm_ref, xp_ref, xn_ref, mod_ref, g_ref, wu_ref, wr_ref, gb_ref, cw_ref, cb_ref, qw_ref, kw_ref,
         uc_ref, og_ref, v_ref, q_ref, k_ref, gt_ref) = refs
    else:
        (xm_ref, xp_ref, xn_ref, mod_ref, g_ref, wu_ref, wr_ref, gb_ref, cw_ref, cb_ref, kw_ref,
         v_ref, k_ref, gt_ref) = refs
    i = pl.program_id(1)
    shift, scale = mod_ref[0:1, :], mod_ref[1:2, :]
    g = g_ref[...]

    def pre(xv):
        return _rms(xv, g) * (1.0 + scale) + shift

    hm = pre(xm_ref[...])
    hp = jnp.where(i > 0, pre(xp_ref[...]), 0.0)
    hn = jnp.where(i < nt - 1, pre(xn_ref[...]), 0.0)
    hext = jnp.concatenate([hp, hm, hn], axis=0).astype(BF16)
    u = _dot(hext, wu_ref[...])
    uc = _conv3(u, cw_ref[...], cb_ref[...], tm)
    uc = uc * _sigmoid(uc)
    ucb = uc.astype(BF16)
    rest = _dot(hm.astype(BF16), wr_ref[...])
    dv = v_ref.shape[1]
    v_ref[...] = rest[:, :dv].astype(BF16)
    if full:
        og_ref[...] = rest[:, dv:2 * dv].astype(BF16)
        uc_ref[...] = ucb
        gt_ref[...] = rest[:, 2 * dv:] + gb_ref[...]
    else:
        gt_ref[...] = rest[:, dv:] + gb_ref[...]
    dh = MLSTM_HEAD_DIM
    for hh in range(MLSTM_HEADS):
        uh = ucb[:, hh * dh:(hh + 1) * dh]
        k_ref[:, hh * dh:(hh + 1) * dh] = _dot(uh, kw_ref[hh]).astype(BF16)
        if full:
            q_ref[:, hh * dh:(hh + 1) * dh] = _dot(uh, qw_ref[hh]).astype(BF16)


def _in1(x, mod, mod_row, g, wu, wr, gb, cw, cb, qw, kw, full):
    bsz, t, d = x.shape
    tm = min(t, 512)
    nt = t // tm
    dw = wu.shape[1]
    row = (lambda b, i: (b, 0, 0)) if mod_row is None else (lambda b, i: (mod_row, 0, 0))
    in_specs = _halo_specs(tm, t, d) + [pl.BlockSpec((None, 6, d), row)]
    consts = [g, wu, wr, gb, cw, cb] + ([qw] if full else []) + [kw]
    in_specs += [_const_spec(a.shape) for a in consts]
    act = lambda: jax.ShapeDtypeStruct((bsz, t, dw), BF16)
    gates = jax.ShapeDtypeStruct((bsz, t, LANES), F32)
    out_shape = [act(), act(), act(), act(), act(), gates] if full else [act(), act(), gates]
    out_specs = [pl.BlockSpec((None, tm, s.shape[2]), lambda b, i: (b, i, 0)) for s in out_shape]
    return pl.pallas_call(
        functools.partial(_in1_kernel, tm=tm, nt=nt, full=full),
        out_shape=out_shape,
        grid=(bsz, nt),
        in_specs=in_specs,
        out_specs=out_specs,
        compiler_params=_params("parallel", "parallel"),
        name="in1_full" if full else "in1_ctx",
    )(x, x, x, mod, *consts)


def _log_sigmoid(x):
    return jnp.minimum(x, 0.0) - jnp.log1p(jnp.exp(-jnp.abs(x)))


def _scan_lanes(x, op, fill, reverse):
    n = x.shape[1]
    lane = lax.broadcasted_iota(jnp.int32, x.shape, 1)
    d = 1
    while d < n:
        if reverse:
            shifted = jnp.where(lane < n - d, pltpu.roll(x, n - d, 1), fill)
        else:
            shifted = jnp.where(lane >= d, pltpu.roll(x, d, 1), fill)
        x = op(x, shifted)
        d *= 2
    return x


def _mlstm_kernel(q_ref, k_ref, v_ref, kc_ref, vc_ref, gl_ref, gc_ref, ng_ref, hs_ref,
                  cf_ref, cb_ref, nf_ref, nb_ref, *, nc):
    ln = MLSTM_CHUNK
    nr = nc + 1
    gates = [jnp.concatenate([gc_ref[gi], gl_ref[gi]], axis=0) for gi in range(4)]

    def direction(ig, fpre, reverse):
        b = _scan_lanes(_log_sigmoid(fpre), jnp.add, 0.0, reverse)
        a = ig - b
        cmax = _scan_lanes(a, jnp.maximum, NEG, reverse)
        e = 0 if reverse else ln - 1
        b_end, a_max = b[:, e:e + 1], cmax[:, e:e + 1]
        order = [0] + (list(range(nc, 0, -1)) if reverse else list(range(1, nr)))
        m_in = [None] * nr
        m = jnp.zeros((1, 1), F32)
        for r in order:
            m_in[r] = m
            m = b_end[r:r + 1] + jnp.maximum(m, a_max[r:r + 1])
        m_in = jnp.concatenate(m_in, axis=0)
        big_m = jnp.maximum(m_in, cmax)
        m_end = jnp.maximum(m_in, a_max)
        return dict(a=a, M=big_m, iw=jnp.exp(m_in - big_m), emt=jnp.exp(-(b + big_m)),
                    wts=jnp.exp(a - m_end), decay=jnp.exp(m_in - m_end))

    fw = direction(gates[0], gates[1], False)
    bw = direction(gates[2], gates[3], True)

    def columns(r):
        tile = jnp.concatenate([fw["M"][r:r + 1], fw["iw"][r:r + 1], fw["emt"][r:r + 1], fw["wts"][r:r + 1],
                                bw["M"][r:r + 1], bw["iw"][r:r + 1], bw["emt"][r:r + 1], bw["wts"][r:r + 1]], axis=0)
        return jnp.transpose(tile)

    cols = [columns(r) for r in range(nr)]

    def chunk_kv(r):
        if r == 0:
            return kc_ref[...], vc_ref[...]
        return k_ref[(r - 1) * ln:r * ln, :], v_ref[(r - 1) * ln:r * ln, :]

    def contrib(r, col):
        kk, vv = chunk_kv(r)
        kw = kk.astype(F32) * cols[r][:, col:col + 1]
        return _dot_tn(kw.astype(BF16), vv), jnp.sum(kw, axis=0, keepdims=True)

    ct, nv = contrib(0, 3)
    for c in range(nc):
        cf_ref[c] = ct.astype(BF16)
        nf_ref[c] = nv
        if c + 1 < nc:
            u_c, n_c = contrib(c + 1, 3)
            dec = fw["decay"][c + 1:c + 2]
            ct = dec * ct + u_c
            nv = dec * nv + n_c
    ct, nv = contrib(0, 7)
    for c in range(nc - 1, -1, -1):
        cb_ref[c] = ct.astype(BF16)
        nb_ref[c] = nv
        if c > 0:
            u_c, n_c = contrib(c + 1, 7)
            dec = bw["decay"][c + 1:c + 2]
            ct = dec * ct + u_c
            nv = dec * nv + n_c

    ti = lax.broadcasted_iota(jnp.int32, (ln, ln), 0)
    sj = lax.broadcasted_iota(jnp.int32, (ln, ln), 1)
    ng = ng_ref[...]
    for c in range(nc):
        r = c + 1
        qc = q_ref[c * ln:(c + 1) * ln, :]
        kk, vv = chunk_kv(r)
        s = _dot_nt(qc, kk)
        col = cols[r]
        qf = qc.astype(F32)
        lhs = []
        p_sum = None
        for dirn, o, causal, n_ref in ((fw, 0, sj <= ti, nf_ref), (bw, 4, sj >= ti, nb_ref)):
            e = jnp.exp(jnp.where(causal, dirn["a"][r:r + 1] - col[:, o:o + 1], NEG))
            p = s * e
            iw = col[:, o + 1:o + 2]
            den = jnp.sum(p, axis=-1, keepdims=True) + iw * jnp.sum(qf * n_ref[c], axis=-1, keepdims=True)
            rinv = 1.0 / jnp.maximum(jnp.abs(den), col[:, o + 2:o + 3])
            p_sum = p * rinv if p_sum is None else p_sum + p * rinv
            lhs.append((qf * (iw * rinv)).astype(BF16))
        lhs = jnp.concatenate([p_sum.astype(BF16)] + lhs, axis=1)
        rhs = jnp.concatenate([vv, cf_ref[c], cb_ref[c]], axis=0)
        hsum = _dot(lhs, rhs)
        hs_ref[c * ln:(c + 1) * ln, :] = (_rms(hsum, ng)).astype(BF16)


def _mlstm(q, k, v, kc, vc, gl, gc, ng):
    bsz, t, wd = q.shape
    dh = MLSTM_HEAD_DIM
    nh = wd // dh
    nc = t // MLSTM_CHUNK
    n_ctx = kc.shape[1]
    assert n_ctx == MLSTM_CHUNK and gl.shape == (bsz, nh, 4, nc, MLSTM_CHUNK)
    seq = lambda n: pl.BlockSpec((None, n, dh), lambda b, h: (b, 0, h))
    return pl.pallas_call(
        functools.partial(_mlstm_kernel, nc=nc),
        out_shape=jax.ShapeDtypeStruct((bsz, t, wd), BF16),
        grid=(bsz, nh),
        in_specs=[seq(t), seq(t), seq(t), seq(n_ctx), seq(n_ctx),
                  pl.BlockSpec((None, None, 4, nc, MLSTM_CHUNK), lambda b, h: (b, h, 0, 0, 0)),
                  pl.BlockSpec((None, None, 4, 1, MLSTM_CHUNK), lambda b, h: (b, h, 0, 0, 0)),
                  pl.BlockSpec((1, dh), lambda b, h: (0, h))],
        out_specs=seq(t),
        scratch_shapes=[pltpu.VMEM((nc, dh, dh), BF16), pltpu.VMEM((nc, dh, dh), BF16),
                        pltpu.VMEM((nc, 1, dh), F32), pltpu.VMEM((nc, 1, dh), F32)],
        compiler_params=_params("parallel", "parallel"),
        name="mlstm",
    )(q, k, v, kc, vc, gl, gc, ng)


def _out1_kernel(x_ref, hs_ref, og_ref, uc_ref, skip_ref, wo_ref, g_ref, mod_ref, o_ref):
    y = _sigmoid(og_ref[...].astype(F32)) * (hs_ref[...].astype(F32) + skip_ref[...] * uc_ref[...].astype(F32))
    y = _dot(y.astype(BF16), wo_ref[...])
    o_ref[...] = x_ref[...] + mod_ref[2:3, :] * _rms(y, g_ref[...])


def _out1(x, hs, og, uc, skip, w_out, g, mod):
    bsz, t, d = x.shape
    tm = min(t, 512)
    wd = hs.shape[2]
    tile = lambda n: pl.BlockSpec((None, tm, n), lambda b, i: (b, i, 0))
    return pl.pallas_call(
        _out1_kernel,
        out_shape=jax.ShapeDtypeStruct((bsz, t, d), F32),
        grid=(bsz, t // tm),
        in_specs=[tile(d), tile(wd), tile(wd), tile(wd), _const_spec(skip.shape), _const_spec(w_out.shape),
                  _const_spec(g.shape), pl.BlockSpec((None, 6, d), lambda b, i: (b, 0, 0))],
        out_specs=tile(d),
        compiler_params=_params("parallel", "parallel"),
        name="out1",
    )(x, hs, og, uc, skip, w_out, g, mod)


def _rope_tables(t_len, scale):
    rows = t_len // GRID_W
    row = jnp.repeat(jnp.arange(rows, dtype=F32), GRID_W)
    col = jnp.tile(jnp.arange(GRID_W, dtype=F32), rows)
    n_freq = HEAD_DIM // 4
    inv = ROPE_BASE ** (-jnp.arange(n_freq, dtype=F32) / n_freq)
    ar, ac = row[:, None] * inv, col[:, None] * inv
    cos = jnp.concatenate([jnp.cos(ar), jnp.cos(ar), jnp.cos(ac), jnp.cos(ac)], axis=-1)
    sin = jnp.concatenate([-jnp.sin(ar), jnp.sin(ar), -jnp.sin(ac), jnp.sin(ac)], axis=-1)
    reps = LANES // HEAD_DIM
    return jnp.tile(cos, (1, reps)) * scale, jnp.tile(sin, (1, reps)) * scale


def _ffn_weights(w_up, conv_w, conv_b, w_down, ck=256):
    d, n2 = w_up.shape
    nch = n2 // ck
    wup3 = w_up.astype(BF16).reshape(d, nch, ck).transpose(1, 0, 2)
    cw3 = conv_w.reshape(conv_w.shape[0], nch, ck).transpose(1, 0, 2)
    cb3 = conv_b.reshape(nch, 1, ck)
    wdn3 = w_down.astype(BF16).reshape(nch // 2, ck, w_down.shape[1])
    return wup3, cw3, cb3, wdn3


def kernel(x, c, ctx, c_ctx, mod_w, mod_b, norm_g, attn_in_w, attn_sink, pool_w, pool_scale, attn_out_w,
           rec_in_w, rec_gate_b, rec_conv_w, rec_conv_b, rec_q_w, rec_k_w, rec_norm_g, rec_skip, rec_out_w,
           ffn_up_w, ffn_conv_w, ffn_conv_b, ffn_down_w):
    bsz, t, d = x.shape
    n_ctx = ctx.shape[1]
    ctx_row = bsz

    pad_rows = (-(bsz + 1)) % 16
    c_all = jnp.concatenate([c, c_ctx[None, :], jnp.zeros((pad_rows, d), F32)], axis=0)
    mod = _adaln(c_all, mod_w, mod_b)
    mod = mod.reshape(mod.shape[0], mod.shape[1], 6, d)
    row2 = lambda a: a.reshape(1, -1)

    w = attn_in_w[0]
    aw, kvw = ATTN_Q_HEADS * HEAD_DIM, HEAD_DIM
    dup = lambda m: jnp.concatenate([m[:, :kvw], m[:, :kvw], m[:, kvw:], m[:, kvw:]], axis=1)
    w_in0 = jnp.concatenate([w[:, :aw], dup(w[:, aw:aw + 2 * kvw]), dup(w[:, aw + 2 * kvw:aw + 4 * kvw]),
                             w[:, aw + 4 * kvw:]], axis=1).astype(BF16)
    qscale = HEAD_DIM ** -0.5
    tables = _rope_tables(t, qscale) + _rope_tables(t, 1.0)
    g00 = row2(norm_g[0, 0])
    q, k4, v4, u = _in0(x, mod[0], None, g00, w_in0, tables)
    qc, kc4, vc4, ucx = _in0(ctx, mod[0], ctx_row, g00, w_in0, None)
    pw = pool_w[0].astype(BF16)
    psc = row2(pool_scale[0])
    wo0 = attn_out_w[0].astype(BF16)
    g01 = row2(norm_g[0, 1])
    sink = attn_sink[0]
    x1 = _mix0(x, q, k4, v4, kc4, vc4, u, sink, pw, psc, wo0, g01, mod[0], None)
    c1 = _mix0(ctx, qc, None, None, kc4, vc4, ucx, sink, pw, psc, wo0, g01, mod[0], ctx_row)
    f0 = _ffn_weights(ffn_up_w[0], ffn_conv_w[0], ffn_conv_b[0], ffn_down_w[0])
    g02, g03 = row2(norm_g[0, 2]), row2(norm_g[0, 3])
    x2 = _ffn(x1, mod[0], None, g02, g03, *f0)
    c2 = _ffn(c1, mod[0], ctx_row, g02, g03, *f0)

    w = rec_in_w[0]
    wd = MLSTM_HEADS * MLSTM_HEAD_DIM
    ng = 4 * MLSTM_HEADS
    gcol = w[:, 3 * wd:].reshape(d, 4, MLSTM_HEADS).transpose(0, 2, 1).reshape(d, ng)
    gcol = jnp.concatenate([gcol, jnp.zeros((d, LANES - ng), F32)], axis=1)
    gbias = jnp.concatenate([rec_gate_b[0].T.reshape(1, ng), jnp.zeros((1, LANES - ng), F32)], axis=1)
    wu = w[:, :wd].astype(BF16)
    wr_full = jnp.concatenate([w[:, wd:3 * wd], gcol], axis=1).astype(BF16)
    wr_ctx = jnp.concatenate([w[:, wd:2 * wd], gcol], axis=1).astype(BF16)
    cw, cb = rec_conv_w[0], row2(rec_conv_b[0])
    qw = rec_q_w[0].astype(BF16)
    kw = (rec_k_w[0] * (MLSTM_HEAD_DIM ** -0.5)).astype(BF16)
    g10 = row2(norm_g[1, 0])
    uc, og, v, qm, km, gts = _in1(x2, mod[1], None, g10, wu, wr_full, gbias, cw, cb, qw, kw, True)
    vcx, kcx, gtc = _in1(c2, mod[1], ctx_row, g10, wu, wr_ctx, gbias, cw, cb, None, kw, False)

    def gate_rows(gt, n):
        gt = gt[:, :, :ng].reshape(bsz, n // MLSTM_CHUNK, MLSTM_CHUNK, MLSTM_HEADS, 4)
        return gt.transpose(0, 3, 4, 1, 2)

    hs = _mlstm(qm, km, v, kcx, vcx, gate_rows(gts, t), gate_rows(gtc, n_ctx), row2(rec_norm_g[0]))
    x3 = _out1(x2, hs, og, uc, row2(rec_skip[0]), rec_out_w[0].astype(BF16), row2(norm_g[1, 1]), mod[1])
    f1 = _ffn_weights(ffn_up_w[1], ffn_conv_w[1], ffn_conv_b[1], ffn_down_w[1])
    return _ffn(x3, mod[1], None, row2(norm_g[1, 2]), row2(norm_g[1, 3]), *f1)
```

```python
import functools

import jax
import jax.numpy as jnp
from jax import lax
from jax.experimental import pallas as pl
from jax.experimental.pallas import tpu as pltpu

F32 = jnp.float32
BF16 = jnp.bfloat16

GRID_W = 64
HEAD_DIM = 64
ATTN_Q_HEADS = 8
ATTN_KV_HEADS = 2
ATTN_BLOCK = 128
ROPE_BASE = 10000.0
POOL_WINDOWS = (2, 4, 8, 16)
POOL_CH = 128
MLSTM_HEADS = 4
MLSTM_HEAD_DIM = 256
MLSTM_CHUNK = 256
EPS = 1e-6
NEG = -1e30

LANES = 128
SUBLANES = 8
VMEM_LIMIT = 56 * 1024 * 1024


def _params(*sem):
    return pltpu.CompilerParams(dimension_semantics=sem, vmem_limit_bytes=VMEM_LIMIT)


def _rms(x, g):
    return x * lax.rsqrt(jnp.mean(x * x, axis=-1, keepdims=True) + EPS) * g


def _sigmoid(x):
    return 1.0 / (1.0 + jnp.exp(-x))


def _dot(a, b):
    return jnp.dot(a, b, preferred_element_type=F32)


def _dot_nt(a, b):
    return lax.dot_general(a, b, (((1,), (1,)), ((), ())), preferred_element_type=F32)


def _dot_tn(a, b):
    return lax.dot_general(a, b, (((0,), (0,)), ((), ())), preferred_element_type=F32)


def _const_spec(shape):
    nd = len(shape)
    return pl.BlockSpec(shape, lambda *_: (0,) * nd)


def _adaln_kernel(c_ref, w_ref, b_ref, o_ref):
    c = c_ref[...]
    s = c * _sigmoid(c)
    o_ref[...] = _dot(s.astype(BF16), w_ref[...].astype(BF16)) + b_ref[...]


def _adaln(c_all, mod_w, mod_b):
    depth, d, n6 = mod_w.shape
    rows = c_all.shape[0]
    tn = 1536
    return pl.pallas_call(
        _adaln_kernel,
        out_shape=jax.ShapeDtypeStruct((depth, rows, n6), F32),
        grid=(depth, n6 // tn),
        in_specs=[
            pl.BlockSpec((rows, d), lambda l, j: (0, 0)),
            pl.BlockSpec((None, d, tn), lambda l, j: (l, 0, j)),
            pl.BlockSpec((None, 1, tn), lambda l, j: (l, 0, j)),
        ],
        out_specs=pl.BlockSpec((None, rows, tn), lambda l, j: (l, 0, j)),
        compiler_params=_params("arbitrary", "arbitrary"),
        name="adaln",
    )(c_all, mod_w, mod_b.reshape(depth, 1, n6))


def _in0_kernel(*refs, rope):
    if rope:
        x_ref, mod_ref, g_ref, w_ref, cq_ref, sq_ref, ck_ref, sk_ref, q_ref, k_ref, v_ref, u_ref = refs
    else:
        x_ref, mod_ref, g_ref, w_ref, q_ref, k_ref, v_ref, u_ref = refs
    x = x_ref[...]
    h = _rms(x, g_ref[...]) * (1.0 + mod_ref[1:2, :]) + mod_ref[0:1, :]
    y = _dot(h.astype(BF16), w_ref[...])
    nq = q_ref.shape[1]
    nk = k_ref.shape[1]
    if rope:
        tm = x.shape[0]
        lane = lax.broadcasted_iota(jnp.int32, (tm, LANES), 1)
        even = (lane // 16) % 2 == 0

        def rot(a, c, s):
            sw = jnp.where(even, pltpu.roll(a, LANES - 16, 1), pltpu.roll(a, 16, 1))
            return a * c + sw * s

        cq, sq, ck, sk = cq_ref[...], sq_ref[...], ck_ref[...], sk_ref[...]
        for j in range(nq // LANES):
            q_ref[:, j * LANES:(j + 1) * LANES] = rot(y[:, j * LANES:(j + 1) * LANES], cq, sq).astype(BF16)
        for j in range(nk // LANES):
            o = nq + j * LANES
            k_ref[:, j * LANES:(j + 1) * LANES] = rot(y[:, o:o + LANES], ck, sk).astype(BF16)
    else:
        q_ref[...] = (y[:, :nq] * (HEAD_DIM ** -0.5)).astype(BF16)
        k_ref[...] = y[:, nq:nq + nk].astype(BF16)
    v_ref[...] = y[:, nq + nk:nq + 2 * nk].astype(BF16)
    u_ref[...] = y[:, nq + 2 * nk:].astype(BF16)


def _in0(x, mod, mod_row, g, w, tables):
    bsz, t, d = x.shape
    tm = min(t, 512)
    rope = tables is not None
    nq, nk, nu = 512, 256, 512
    row = (lambda b, i: (b, 0, 0)) if mod_row is None else (lambda b, i: (mod_row, 0, 0))
    in_specs = [
        pl.BlockSpec((None, tm, d), lambda b, i: (b, i, 0)),
        pl.BlockSpec((None, 6, d), row),
        _const_spec((1, d)),
        _const_spec(w.shape),
    ]
    args = [x, mod, g, w]
    if rope:
        in_specs += [pl.BlockSpec((tm, LANES), lambda b, i: (i, 0))] * 4
        args += list(tables)
    outs = [(nq, BF16), (nk, BF16), (nk, BF16), (nu, BF16)]
    return pl.pallas_call(
        functools.partial(_in0_kernel, rope=rope),
        out_shape=[jax.ShapeDtypeStruct((bsz, t, n), dt) for n, dt in outs],
        grid=(bsz, t // tm),
        in_specs=in_specs,
        out_specs=[pl.BlockSpec((None, tm, n), lambda b, i: (b, i, 0)) for n, _ in outs],
        compiler_params=_params("parallel", "parallel"),
        name="in0_rope" if rope else "in0_ctx",
    )(*args)


def _mix0_kernel(*refs, band, t_len, tq):
    if band:
        (sink_ref, x_ref, q_ref, k_ref, v_ref, kc_ref, vc_ref, u_ref, pw_ref, ps_ref, wo_ref, g_ref, mod_ref,
         o_ref, cat_ref) = refs
    else:
        (sink_ref, x_ref, q_ref, kc_ref, vc_ref, u_ref, pw_ref, ps_ref, wo_ref, g_ref, mod_ref,
         o_ref, cat_ref) = refs
    blk = ATTN_BLOCK
    nb = t_len // blk
    nsub = tq // blk
    tstep = pl.program_id(1)
    n_ctx = kc_ref.shape[0]
    n_keys = (3 * blk if band else 0) + n_ctx

    qi = lax.broadcasted_iota(jnp.int32, (blk, n_keys), 0)
    kj = lax.broadcasted_iota(jnp.int32, (blk, n_keys), 1)
    if band:
        in_band = (kj >= qi) & (kj <= qi + 2 * blk)
        is_ctx = kj >= 3 * blk
    lane_k = lax.broadcasted_iota(jnp.int32, (n_keys, LANES), 1) < HEAD_DIM
    lane_q = lax.broadcasted_iota(jnp.int32, (blk, LANES), 1) < HEAD_DIM
    row3 = lax.broadcasted_iota(jnp.int32, (blk, 3 * blk), 0)
    col3 = lax.broadcasted_iota(jnp.int32, (blk, 3 * blk), 1) - blk
    pool_band = [((col3 >= row3 - w // 2) & (col3 <= row3 + w // 2 - 1)).astype(BF16) for w in POOL_WINDOWS]
    row_pos = lax.broadcasted_iota(jnp.int32, (blk, LANES), 0)

    for j in range(nsub):
        n = tstep * nsub + j
        r0 = pl.multiple_of(n * blk, blk)
        ps = pl.multiple_of(jnp.maximum(n - 1, 0) * blk, blk)
        ns = pl.multiple_of(jnp.minimum(n + 1, nb - 1) * blk, blk)
        has_prev = n > 0
        has_next = n < nb - 1
        rows = slice(j * blk, (j + 1) * blk)

        if band:
            kb = jnp.concatenate([k_ref[pl.ds(ps, blk), :], k_ref[pl.ds(r0, blk), :], k_ref[pl.ds(ns, blk), :],
                                  kc_ref[...]], axis=0)
            vb = jnp.concatenate([v_ref[pl.ds(ps, blk), :], v_ref[pl.ds(r0, blk), :], v_ref[pl.ds(ns, blk), :],
                                  vc_ref[...]], axis=0)
            valid = is_ctx | (in_band & ((kj >= blk) | has_prev) & ((kj < 2 * blk) | has_next))
        else:
            kb = kc_ref[...]
            vb = vc_ref[...]
            valid = None
        zero = jnp.zeros((), BF16)
        for hk in range(ATTN_KV_HEADS):
            kk = kb[:, hk * LANES:(hk + 1) * LANES]
            vv = vb[:, hk * LANES:(hk + 1) * LANES]
            k_lo = jnp.where(lane_k, kk, zero)
            k_hi = jnp.where(lane_k, zero, kk)
            v_cat = jnp.concatenate([jnp.where(lane_k, vv, zero), jnp.where(lane_k, zero, vv)], axis=0)
            for pair in range(2):
                c = hk * 2 + pair
                qc = q_ref[rows, c * LANES:(c + 1) * LANES]
                probs, inv = [], []
                for half, kh in enumerate((k_lo, k_hi)):
                    s = _dot_nt(qc, kh)
                    if valid is not None:
                        s = jnp.where(valid, s, NEG)
                    snk = sink_ref[2 * c + half]
                    m = jnp.maximum(jnp.max(s, axis=-1, keepdims=True), snk)
                    p = jnp.exp(s - m)
                    den = jnp.sum(p, axis=-1, keepdims=True) + jnp.exp(snk - m)
                    probs.append(p.astype(BF16))
                    inv.append(1.0 / den)
                o = _dot(jnp.concatenate(probs, axis=1), v_cat)
                o = o * jnp.where(lane_q, inv[0], inv[1])
                cat_ref[rows, c * LANES:(c + 1) * LANES] = o.astype(BF16)

        uo = u_ref[pl.ds(r0, blk), :]
        up = jnp.where(has_prev, u_ref[pl.ds(ps, blk), :], zero)
        un = jnp.where(has_next, u_ref[pl.ds(ns, blk), :], zero)
        ub = jnp.concatenate([up, uo, un], axis=0)
        pos = r0 + row_pos
        for gi, w in enumerate(POOL_WINDOWS):
            cs = slice(gi * POOL_CH, (gi + 1) * POOL_CH)
            sums = _dot(pool_band[gi], ub[:, cs])
            cnt = jnp.minimum(pos + w // 2, t_len) - jnp.maximum(pos - w // 2, 0)
            dlt = sums / cnt.astype(F32) - uo[:, cs].astype(F32)
            y = _dot(dlt.astype(BF16), pw_ref[gi]) * ps_ref[:, cs]
            off = ATTN_Q_HEADS * HEAD_DIM + gi * POOL_CH
            cat_ref[rows, off:off + POOL_CH] = y.astype(BF16)

    y = _dot(cat_ref[...], wo_ref[...])
    o_ref[...] = x_ref[...] + mod_ref[2:3, :] * _rms(y, g_ref[...])


def _mix0(x, q, k4, v4, kc4, vc4, u, sink, pool_w, pool_scale, w_out, g, mod, mod_row):
    bsz, t, d = x.shape
    band = k4 is not None
    tq = min(t, 512)
    n_ctx = kc4.shape[1]
    row = (lambda b, i: (b, 0, 0)) if mod_row is None else (lambda b, i: (mod_row, 0, 0))
    full = lambda n, w: pl.BlockSpec((None, n, w), lambda b, i: (b, 0, 0))
    in_specs = [
        pl.BlockSpec(memory_space=pltpu.SMEM),
        pl.BlockSpec((None, tq, d), lambda b, i: (b, i, 0)),
        pl.BlockSpec((None, tq, q.shape[2]), lambda b, i: (b, i, 0)),
    ]
    args = [sink, x, q]
    if band:
        in_specs += [full(t, k4.shape[2]), full(t, v4.shape[2])]
        args += [k4, v4]
    in_specs += [full(n_ctx, kc4.shape[2]), full(n_ctx, vc4.shape[2]), full(t, u.shape[2]),
                 _const_spec(pool_w.shape), _const_spec(pool_scale.shape), _const_spec(w_out.shape),
                 _const_spec(g.shape), pl.BlockSpec((None, 6, d), row)]
    args += [kc4, vc4, u, pool_w, pool_scale, w_out, g, mod]
    return pl.pallas_call(
        functools.partial(_mix0_kernel, band=band, t_len=t, tq=tq),
        out_shape=jax.ShapeDtypeStruct((bsz, t, d), F32),
        grid=(bsz, t // tq),
        in_specs=in_specs,
        out_specs=pl.BlockSpec((None, tq, d), lambda b, i: (b, i, 0)),
        scratch_shapes=[pltpu.VMEM((tq, w_out.shape[0]), BF16)],
        compiler_params=_params("parallel", "parallel"),
        name="mix0_band" if band else "mix0_ctx",
    )(*args)


HALO = SUBLANES


def _halo_specs(tm, t, d):
    nblk = tm // HALO
    last = t // HALO - 1
    return [
        pl.BlockSpec((None, tm, d), lambda b, i: (b, i, 0)),
        pl.BlockSpec((None, HALO, d), lambda b, i: (b, jnp.maximum(i * nblk - 1, 0), 0)),
        pl.BlockSpec((None, HALO, d), lambda b, i: (b, jnp.minimum((i + 1) * nblk, last), 0)),
    ]


def _conv3(u, cw, cb, tm):
    return (cw[0:1, :] * u[HALO - 1:HALO - 1 + tm, :] + cw[1:2, :] * u[HALO:HALO + tm, :]
            + cw[2:3, :] * u[HALO + 1:HALO + 1 + tm, :] + cb)


def _ffn_kernel(xm_ref, xp_ref, xn_ref, mod_ref, gpre_ref, gpost_ref, wup_ref, cw_ref, cb_ref, wdn_ref,
                o_ref, hb_ref, acc_ref, u0_ref, u1_ref, *, tm, nt):
    i = pl.program_id(1)
    shift, scale, gate = mod_ref[3:4, :], mod_ref[4:5, :], mod_ref[5:6, :]
    gpre = gpre_ref[...]

    def pre(xv):
        return _rms(xv, gpre) * (1.0 + scale) + shift

    hp = jnp.where(i > 0, pre(xp_ref[...]), 0.0)
    hn = jnp.where(i < nt - 1, pre(xn_ref[...]), 0.0)
    hb_ref[...] = jnp.concatenate([hp, pre(xm_ref[...]), hn], axis=0).astype(BF16)
    acc_ref[...] = jnp.zeros_like(acc_ref)
    nchunk = wdn_ref.shape[0]
    ncol = wup_ref.shape[2] // LANES
    half = ncol // 2
    u_refs = (u0_ref, u1_ref)

    def up(c, s):
        u = _dot(hb_ref[...], wup_ref[c])
        for j in range(ncol):
            u_refs[s][j] = u[:, j * LANES:(j + 1) * LANES]

    def conv_act_down(c, s):
        cw, cb, ur = cw_ref[c], cb_ref[c], u_refs[s]

        def conv(j):
            cs = slice(j * LANES, (j + 1) * LANES)
            return (cw[0:1, cs] * ur[j, pl.ds(HALO - 1, tm), :] + cw[1:2, cs] * ur[j, pl.ds(HALO, tm), :]
                    + cw[2:3, cs] * ur[j, pl.ds(HALO + 1, tm), :] + cb[:, cs])

        acts = []
        for j in range(half):
            ag, av = conv(j), conv(half + j)
            acts.append((ag * _sigmoid(ag) * av).astype(BF16))
        acc_ref[...] += _dot(jnp.concatenate(acts, axis=1), wdn_ref[c])

    up(0, 0)
    for c in range(nchunk):
        if c + 1 < nchunk:
            up(c + 1, (c + 1) % 2)
        conv_act_down(c, c % 2)
    o_ref[...] = xm_ref[...] + gate * _rms(acc_ref[...], gpost_ref[...])


def _ffn(x, mod, mod_row, gpre, gpost, wup3, cw3, cb3, wdn3):
    bsz, t, d = x.shape
    tm = min(t, 512)
    nt = t // tm
    ck = wup3.shape[2]
    row = (lambda b, i: (b, 0, 0)) if mod_row is None else (lambda b, i: (mod_row, 0, 0))
    in_specs = _halo_specs(tm, t, d) + [
        pl.BlockSpec((None, 6, d), row), _const_spec(gpre.shape), _const_spec(gpost.shape),
        _const_spec(wup3.shape), _const_spec(cw3.shape), _const_spec(cb3.shape), _const_spec(wdn3.shape)]
    u_buf = pltpu.VMEM((ck // LANES, tm + 2 * HALO, LANES), F32)
    return pl.pallas_call(
        functools.partial(_ffn_kernel, tm=tm, nt=nt),
        out_shape=jax.ShapeDtypeStruct((bsz, t, d), F32),
        grid=(bsz, nt),
        in_specs=in_specs,
        out_specs=pl.BlockSpec((None, tm, d), lambda b, i: (b, i, 0)),
        scratch_shapes=[pltpu.VMEM((tm + 2 * HALO, d), BF16), pltpu.VMEM((tm, d), F32), u_buf, u_buf],
        compiler_params=_params("parallel", "parallel"),
        name="ffn",
    )(x, x, x, mod, gpre, gpost, wup3, cw3, cb3, wdn3)


def _in1_kernel(*refs, tm, nt, full):
    if full:
        (xm_ref, xp_ref, xn_ref, mod_ref, g_ref, wu_ref, wr_ref, gb_ref, cw_ref, cb_ref, qw_ref, kw_ref,
         uc_ref, og_ref, v_ref, q_ref, k_ref, gt_ref) = refs
    else:
        (xm_ref, xp_ref, xn_ref, mod_ref, g_ref, wu_ref, wr_ref, gb_ref, cw_ref, cb_ref, kw_ref,
         v_ref, k_ref, gt_ref) = refs
    i = pl.program_id(1)
    shift, scale = mod_ref[0:1, :], mod_ref[1:2, :]
    g = g_ref[...]

    def pre(xv):
        return _rms(xv, g) * (1.0 + scale) + shift

    hm = pre(xm_ref[...])
    hp = jnp.where(i > 0, pre(xp_ref[...]), 0.0)
    hn = jnp.where(i < nt - 1, pre(xn_ref[...]), 0.0)
    hext = jnp.concatenate([hp, hm, hn], axis=0).astype(BF16)
    u = _dot(hext, wu_ref[...])
    uc = _conv3(u, cw_ref[...], cb_ref[...], tm)
    uc = uc * _sigmoid(uc)
    ucb = uc.astype(BF16)
    rest = _dot(hm.astype(BF16), wr_ref[...])
    dv = v_ref.shape[1]
    v_ref[...] = rest[:, :dv].astype(BF16)
    if full:
        og_ref[...] = rest[:, dv:2 * dv].astype(BF16)
        uc_ref[...] = ucb
        gt_ref[...] = rest[:, 2 * dv:] + gb_ref[...]
    else:
        gt_ref[...] = rest[:, dv:] + gb_ref[...]
    dh = MLSTM_HEAD_DIM
    for hh in range(MLSTM_HEADS):
        uh = ucb[:, hh * dh:(hh + 1) * dh]
        k_ref[:, hh * dh:(hh + 1) * dh] = _dot(uh, kw_ref[hh]).astype(BF16)
        if full:
            q_ref[:, hh * dh:(hh + 1) * dh] = _dot(uh, qw_ref[hh]).astype(BF16)


def _in1(x, mod, mod_row, g, wu, wr, gb, cw, cb, qw, kw, full):
    bsz, t, d = x.shape
    tm = min(t, 512)
    nt = t // tm
    dw = wu.shape[1]
    row = (lambda b, i: (b, 0, 0)) if mod_row is None else (lambda b, i: (mod_row, 0, 0))
    in_specs = _halo_specs(tm, t, d) + [pl.BlockSpec((None, 6, d), row)]
    consts = [g, wu, wr, gb, cw, cb] + ([qw] if full else []) + [kw]
    in_specs += [_const_spec(a.shape) for a in consts]
    act = lambda: jax.ShapeDtypeStruct((bsz, t, dw), BF16)
    gates = jax.ShapeDtypeStruct((bsz, t, LANES), F32)
    out_shape = [act(), act(), act(), act(), act(), gates] if full else [act(), act(), gates]
    out_specs = [pl.BlockSpec((None, tm, s.shape[2]), lambda b, i: (b, i, 0)) for s in out_shape]
    return pl.pallas_call(
        functools.partial(_in1_kernel, tm=tm, nt=nt, full=full),
        out_shape=out_shape,
        grid=(bsz, nt),
        in_specs=in_specs,
        out_specs=out_specs,
        compiler_params=_params("parallel", "parallel"),
        name="in1_full" if full else "in1_ctx",
    )(x, x, x, mod, *consts)


def _log_sigmoid(x):
    return jnp.minimum(x, 0.0) - jnp.log1p(jnp.exp(-jnp.abs(x)))


def _scan_lanes(x, op, fill, reverse):
    n = x.shape[1]
    lane = lax.broadcasted_iota(jnp.int32, x.shape, 1)
    d = 1
    while d < n:
        if reverse:
            shifted = jnp.where(lane < n - d, pltpu.roll(x, n - d, 1), fill)
        else:
            shifted = jnp.where(lane >= d, pltpu.roll(x, d, 1), fill)
        x = op(x, shifted)
        d *= 2
    return x


def _mlstm_kernel(q_ref, k_ref, v_ref, kc_ref, vc_ref, gl_ref, gc_ref, ng_ref, hs_ref,
                  cf_ref, cb_ref, nf_ref, nb_ref, *, nc):
    ln = MLSTM_CHUNK
    nr = nc + 1
    gates = [jnp.concatenate([gc_ref[gi], gl_ref[gi]], axis=0) for gi in range(4)]

    def direction(ig, fpre, reverse):
        b = _scan_lanes(_log_sigmoid(fpre), jnp.add, 0.0, reverse)
        a = ig - b
        cmax = _scan_lanes(a, jnp.maximum, NEG, reverse)
        e = 0 if reverse else ln - 1
        b_end, a_max = b[:, e:e + 1], cmax[:, e:e + 1]
        order = [0] + (list(range(nc, 0, -1)) if reverse else list(range(1, nr)))
        m_in = [None] * nr
        m = jnp.zeros((1, 1), F32)
        for r in order:
            m_in[r] = m
            m = b_end[r:r + 1] + jnp.maximum(m, a_max[r:r + 1])
        m_in = jnp.concatenate(m_in, axis=0)
        big_m = jnp.maximum(m_in, cmax)
        m_end = jnp.maximum(m_in, a_max)
        return dict(a=a, M=big_m, iw=jnp.exp(m_in - big_m), emt=jnp.exp(-(b + big_m)),
                    wts=jnp.exp(a - m_end), decay=jnp.exp(m_in - m_end))

    fw = direction(gates[0], gates[1], False)
    bw = direction(gates[2], gates[3], True)

    def columns(r):
        tile = jnp.concatenate([fw["M"][r:r + 1], fw["iw"][r:r + 1], fw["emt"][r:r + 1], fw["wts"][r:r + 1],
                                bw["M"][r:r + 1], bw["iw"][r:r + 1], bw["emt"][r:r + 1], bw["wts"][r:r + 1]], axis=0)
        return jnp.transpose(tile)

    cols = [columns(r) for r in range(nr)]

    def chunk_kv(r):
        if r == 0:
            return kc_ref[...], vc_ref[...]
        return k_ref[(r - 1) * ln:r * ln, :], v_ref[(r - 1) * ln:r * ln, :]

    def contrib(r, col):
        kk, vv = chunk_kv(r)
        kw = kk.astype(F32) * cols[r][:, col:col + 1]
        return _dot_tn(kw.astype(BF16), vv), jnp.sum(kw, axis=0, keepdims=True)

    ct, nv = contrib(0, 3)
    for c in range(nc):
        cf_ref[c] = ct.astype(BF16)
        nf_ref[c] = nv
        if c + 1 < nc:
            u_c, n_c = contrib(c + 1, 3)
            dec = fw["decay"][c + 1:c + 2]
            ct = dec * ct + u_c
            nv = dec * nv + n_c
    ct, nv = contrib(0, 7)
    for c in range(nc - 1, -1, -1):
        cb_ref[c] = ct.astype(BF16)
        nb_ref[c] = nv
        if c > 0:
            u_c, n_c = contrib(c + 1, 7)
            dec = bw["decay"][c + 1:c + 2]
            ct = dec * ct + u_c
            nv = dec * nv + n_c

    ti = lax.broadcasted_iota(jnp.int32, (ln, ln), 0)
    sj = lax.broadcasted_iota(jnp.int32, (ln, ln), 1)
    ng = ng_ref[...]
    for c in range(nc):
        r = c + 1
        qc = q_ref[c * ln:(c + 1) * ln, :]
        kk, vv = chunk_kv(r)
        s = _dot_nt(qc, kk)
        col = cols[r]
        qf = qc.astype(F32)
        lhs = []
        p_sum = None
        for dirn, o, causal, n_ref in ((fw, 0, sj <= ti, nf_ref), (bw, 4, sj >= ti, nb_ref)):
            e = jnp.exp(jnp.where(causal, dirn["a"][r:r + 1] - col[:, o:o + 1], NEG))
            p = s * e
            iw = col[:, o + 1:o + 2]
            den = jnp.sum(p, axis=-1, keepdims=True) + iw * jnp.sum(qf * n_ref[c], axis=-1, keepdims=True)
            rinv = 1.0 / jnp.maximum(jnp.abs(den), col[:, o + 2:o + 3])
            p_sum = p * rinv if p_sum is None else p_sum + p * rinv
            lhs.append((qf * (iw * rinv)).astype(BF16))
        lhs = jnp.concatenate([p_sum.astype(BF16)] + lhs, axis=1)
        rhs = jnp.concatenate([vv, cf_ref[c], cb_ref[c]], axis=0)
        hsum = _dot(lhs, rhs)
        hs_ref[c * ln:(c + 1) * ln, :] = (_rms(hsum, ng)).astype(BF16)


def _mlstm(q, k, v, kc, vc, gl, gc, ng):
    bsz, t, wd = q.shape
    dh = MLSTM_HEAD_DIM
    nh = wd // dh
    nc = t // MLSTM_CHUNK
    n_ctx = kc.shape[1]
    assert n_ctx == MLSTM_CHUNK and gl.shape == (bsz, nh, 4, nc, MLSTM_CHUNK)
    seq = lambda n: pl.BlockSpec((None, n, dh), lambda b, h: (b, 0, h))
    return pl.pallas_call(
        functools.partial(_mlstm_kernel, nc=nc),
        out_shape=jax.ShapeDtypeStruct((bsz, t, wd), BF16),
        grid=(bsz, nh),
        in_specs=[seq(t), seq(t), seq(t), seq(n_ctx), seq(n_ctx),
                  pl.BlockSpec((None, None, 4, nc, MLSTM_CHUNK), lambda b, h: (b, h, 0, 0, 0)),
                  pl.BlockSpec((None, None, 4, 1, MLSTM_CHUNK), lambda b, h: (b, h, 0, 0, 0)),
                  pl.BlockSpec((1, dh), lambda b, h: (0, h))],
        out_specs=seq(t),
        scratch_shapes=[pltpu.VMEM((nc, dh, dh), BF16), pltpu.VMEM((nc, dh, dh), BF16),
                        pltpu.VMEM((nc, 1, dh), F32), pltpu.VMEM((nc, 1, dh), F32)],
        compiler_params=_params("parallel", "parallel"),
        name="mlstm",
    )(q, k, v, kc, vc, gl, gc, ng)


def _out1_kernel(x_ref, hs_ref, og_ref, uc_ref, skip_ref, wo_ref, g_ref, mod_ref, o_ref):
    y = _sigmoid(og_ref[...].astype(F32)) * (hs_ref[...].astype(F32) + skip_ref[...] * uc_ref[...].astype(F32))
    y = _dot(y.astype(BF16), wo_ref[...])
    o_ref[...] = x_ref[...] + mod_ref[2:3, :] * _rms(y, g_ref[...])


def _out1(x, hs, og, uc, skip, w_out, g, mod):
    bsz, t, d = x.shape
    tm = min(t, 512)
    wd = hs.shape[2]
    tile = lambda n: pl.BlockSpec((None, tm, n), lambda b, i: (b, i, 0))
    return pl.pallas_call(
        _out1_kernel,
        out_shape=jax.ShapeDtypeStruct((bsz, t, d), F32),
        grid=(bsz, t // tm),
        in_specs=[tile(d), tile(wd), tile(wd), tile(wd), _const_spec(skip.shape), _const_spec(w_out.shape),
                  _const_spec(g.shape), pl.BlockSpec((None, 6, d), lambda b, i: (b, 0, 0))],
        out_specs=tile(d),
        compiler_params=_params("parallel", "parallel"),
        name="out1",
    )(x, hs, og, uc, skip, w_out, g, mod)


def _rope_tables(t_len, scale):
    rows = t_len // GRID_W
    row = jnp.repeat(jnp.arange(rows, dtype=F32), GRID_W)
    col = jnp.tile(jnp.arange(GRID_W, dtype=F32), rows)
    n_freq = HEAD_DIM // 4
    inv = ROPE_BASE ** (-jnp.arange(n_freq, dtype=F32) / n_freq)
    ar, ac = row[:, None] * inv, col[:, None] * inv
    cos = jnp.concatenate([jnp.cos(ar), jnp.cos(ar), jnp.cos(ac), jnp.cos(ac)], axis=-1)
    sin = jnp.concatenate([-jnp.sin(ar), jnp.sin(ar), -jnp.sin(ac), jnp.sin(ac)], axis=-1)
    reps = LANES // HEAD_DIM
    return jnp.tile(cos, (1, reps)) * scale, jnp.tile(sin, (1, reps)) * scale


def _ffn_weights(w_up, conv_w, conv_b, w_down, ck=256):
    n2 = w_up.shape[1]
    nch = n2 // (2 * ck)
    pair = lambda a: jnp.concatenate([a[..., :n2 // 2].reshape(a.shape[:-1] + (nch, ck)),
                                      a[..., n2 // 2:].reshape(a.shape[:-1] + (nch, ck))], axis=-1)
    wup3 = pair(w_up.astype(BF16)).transpose(1, 0, 2)
    cw3 = pair(conv_w).transpose(1, 0, 2)
    cb3 = pair(conv_b)[:, None, :]
    wdn3 = w_down.astype(BF16).reshape(nch, ck, w_down.shape[1])
    return wup3, cw3, cb3, wdn3


def kernel(x, c, ctx, c_ctx, mod_w, mod_b, norm_g, attn_in_w, attn_sink, pool_w, pool_scale, attn_out_w,
           rec_in_w, rec_gate_b, rec_conv_w, rec_conv_b, rec_q_w, rec_k_w, rec_norm_g, rec_skip, rec_out_w,
           ffn_up_w, ffn_conv_w, ffn_conv_b, ffn_down_w):
    bsz, t, d = x.shape
    n_ctx = ctx.shape[1]
    ctx_row = bsz

    pad_rows = (-(bsz + 1)) % 16
    c_all = jnp.concatenate([c, c_ctx[None, :], jnp.zeros((pad_rows, d), F32)], axis=0)
    mod = _adaln(c_all, mod_w, mod_b)
    mod = mod.reshape(mod.shape[0], mod.shape[1], 6, d)
    row2 = lambda a: a.reshape(1, -1)

    w = attn_in_w[0]
    aw, kvw = ATTN_Q_HEADS * HEAD_DIM, HEAD_DIM
    dup = lambda m: jnp.concatenate([m[:, :kvw], m[:, :kvw], m[:, kvw:], m[:, kvw:]], axis=1)
    w_in0 = jnp.concatenate([w[:, :aw], dup(w[:, aw:aw + 2 * kvw]), dup(w[:, aw + 2 * kvw:aw + 4 * kvw]),
                             w[:, aw + 4 * kvw:]], axis=1).astype(BF16)
    qscale = HEAD_DIM ** -0.5
    tables = _rope_tables(t, qscale) + _rope_tables(t, 1.0)
    g00 = row2(norm_g[0, 0])
    q, k4, v4, u = _in0(x, mod[0], None, g00, w_in0, tables)
    qc, kc4, vc4, ucx = _in0(ctx, mod[0], ctx_row, g00, w_in0, None)
    pw = pool_w[0].astype(BF16)
    psc = row2(pool_scale[0])
    wo0 = attn_out_w[0].astype(BF16)
    g01 = row2(norm_g[0, 1])
    sink = attn_sink[0]
    x1 = _mix0(x, q, k4, v4, kc4, vc4, u, sink, pw, psc, wo0, g01, mod[0], None)
    c1 = _mix0(ctx, qc, None, None, kc4, vc4, ucx, sink, pw, psc, wo0, g01, mod[0], ctx_row)
    f0 = _ffn_weights(ffn_up_w[0], ffn_conv_w[0], ffn_conv_b[0], ffn_down_w[0])
    g02, g03 = row2(norm_g[0, 2]), row2(norm_g[0, 3])
    x2 = _ffn(x1, mod[0], None, g02, g03, *f0)
    c2 = _ffn(c1, mod[0], ctx_row, g02, g03, *f0)

    w = rec_in_w[0]
    wd = MLSTM_HEADS * MLSTM_HEAD_DIM
    ng = 4 * MLSTM_HEADS
    gcol = w[:, 3 * wd:].reshape(d, 4, MLSTM_HEADS).transpose(0, 2, 1).reshape(d, ng)
    gcol = jnp.concatenate([gcol, jnp.zeros((d, LANES - ng), F32)], axis=1)
    gbias = jnp.concatenate([rec_gate_b[0].T.reshape(1, ng), jnp.zeros((1, LANES - ng), F32)], axis=1)
    wu = w[:, :wd].astype(BF16)
    wr_full = jnp.concatenate([w[:, wd:3 * wd], gcol], axis=1).astype(BF16)
    wr_ctx = jnp.concatenate([w[:, wd:2 * wd], gcol], axis=1).astype(BF16)
    cw, cb = rec_conv_w[0], row2(rec_conv_b[0])
    qw = rec_q_w[0].astype(BF16)
    kw = (rec_k_w[0] * (MLSTM_HEAD_DIM ** -0.5)).astype(BF16)
    g10 = row2(norm_g[1, 0])
    uc, og, v, qm, km, gts = _in1(x2, mod[1], None, g10, wu, wr_full, gbias, cw, cb, qw, kw, True)
    vcx, kcx, gtc = _in1(c2, mod[1], ctx_row, g10, wu, wr_ctx, gbias, cw, cb, None, kw, False)

    def gate_rows(gt, n):
        gt = gt[:, :, :ng].reshape(bsz, n // MLSTM_CHUNK, MLSTM_CHUNK, MLSTM_HEADS, 4)
        return gt.transpose(0, 3, 4, 1, 2)

    hs = _mlstm(qm, km, v, kcx, vcx, gate_rows(gts, t), gate_rows(gtc, n_ctx), row2(rec_norm_g[0]))
    x3 = _out1(x2, hs, og, uc, row2(rec_skip[0]), rec_out_w[0].astype(BF16), row2(norm_g[1, 1]), mod[1])
    f1 = _ffn_weights(ffn_up_w[1], ffn_conv_w[1], ffn_conv_b[1], ffn_down_w[1])
    return _ffn(x3, mod[1], None, row2(norm_g[1, 2]), row2(norm_g[1, 3]), *f1)
```

```python
import functools

import jax
import jax.numpy as jnp
from jax import lax
from jax.experimental import pallas as pl
from jax.experimental.pallas import tpu as pltpu

F32 = jnp.float32
BF16 = jnp.bfloat16

GRID_W = 64
HEAD_DIM = 64
ATTN_Q_HEADS = 8
ATTN_KV_HEADS = 2
ATTN_BLOCK = 128
ROPE_BASE = 10000.0
POOL_WINDOWS = (2, 4, 8, 16)
POOL_CH = 128
MLSTM_HEADS = 4
MLSTM_HEAD_DIM = 256
MLSTM_CHUNK = 256
EPS = 1e-6
NEG = -1e30

LANES = 128
SUBLANES = 8
VMEM_LIMIT = 56 * 1024 * 1024


def _params(*sem):
    return pltpu.CompilerParams(dimension_semantics=sem, vmem_limit_bytes=VMEM_LIMIT)


def _rms(x, g):
    return x * lax.rsqrt(jnp.mean(x * x, axis=-1, keepdims=True) + EPS) * g


def _sigmoid(x):
    return 1.0 / (1.0 + jnp.exp(-x))


def _dot(a, b):
    return jnp.dot(a, b, preferred_element_type=F32)


def _dot_nt(a, b):
    return lax.dot_general(a, b, (((1,), (1,)), ((), ())), preferred_element_type=F32)


def _dot_tn(a, b):
    return lax.dot_general(a, b, (((0,), (0,)), ((), ())), preferred_element_type=F32)


def _const_spec(shape):
    nd = len(shape)
    return pl.BlockSpec(shape, lambda *_: (0,) * nd)


def _adaln_kernel(c_ref, w_ref, b_ref, o_ref):
    c = c_ref[...]
    s = c * _sigmoid(c)
    o_ref[...] = _dot(s.astype(BF16), w_ref[...].astype(BF16)) + b_ref[...]


def _adaln(c_all, mod_w, mod_b):
    depth, d, n6 = mod_w.shape
    rows = c_all.shape[0]
    tn = 1536
    return pl.pallas_call(
        _adaln_kernel,
        out_shape=jax.ShapeDtypeStruct((depth, rows, n6), F32),
        grid=(depth, n6 // tn),
        in_specs=[
            pl.BlockSpec((rows, d), lambda l, j: (0, 0)),
            pl.BlockSpec((None, d, tn), lambda l, j: (l, 0, j)),
            pl.BlockSpec((None, 1, tn), lambda l, j: (l, 0, j)),
        ],
        out_specs=pl.BlockSpec((None, rows, tn), lambda l, j: (l, 0, j)),
        compiler_params=_params("arbitrary", "arbitrary"),
        name="adaln",
    )(c_all, mod_w, mod_b.reshape(depth, 1, n6))


def _in0_kernel(*refs, rope):
    if rope:
        x_ref, mod_ref, g_ref, w_ref, cq_ref, sq_ref, ck_ref, sk_ref, q_ref, k_ref, v_ref, u_ref = refs
    else:
        x_ref, mod_ref, g_ref, w_ref, q_ref, k_ref, v_ref, u_ref = refs
    x = x_ref[...]
    h = _rms(x, g_ref[...]) * (1.0 + mod_ref[1:2, :]) + mod_ref[0:1, :]
    y = _dot(h.astype(BF16), w_ref[...])
    nq = q_ref.shape[1]
    nk = v_ref.shape[1]
    tm = x.shape[0]

    def put_keys(kf):
        for jb in range(tm // ATTN_BLOCK):
            k_ref[jb] = jnp.transpose(kf[jb * ATTN_BLOCK:(jb + 1) * ATTN_BLOCK, :]).astype(BF16)

    if rope:
        lane = lax.broadcasted_iota(jnp.int32, (tm, LANES), 1)
        even = (lane // 16) % 2 == 0

        def rot(a, c, s):
            sw = jnp.where(even, pltpu.roll(a, LANES - 16, 1), pltpu.roll(a, 16, 1))
            return a * c + sw * s

        cq, sq, ck, sk = cq_ref[...], sq_ref[...], ck_ref[...], sk_ref[...]
        for j in range(nq // LANES):
            q_ref[:, j * LANES:(j + 1) * LANES] = rot(y[:, j * LANES:(j + 1) * LANES], cq, sq).astype(BF16)
        put_keys(jnp.concatenate([rot(y[:, nq + j * LANES:nq + (j + 1) * LANES], ck, sk)
                                  for j in range(nk // LANES)], axis=1))
    else:
        q_ref[...] = (y[:, :nq] * (HEAD_DIM ** -0.5)).astype(BF16)
        put_keys(y[:, nq:nq + nk])
    v_ref[...] = y[:, nq + nk:nq + 2 * nk].astype(BF16)
    u_ref[...] = y[:, nq + 2 * nk:].astype(BF16)


def _in0(x, mod, mod_row, g, w, tables):
    bsz, t, d = x.shape
    tm = min(t, 512)
    rope = tables is not None
    nq, nk, nu = 512, 256, 512
    row = (lambda b, i: (b, 0, 0)) if mod_row is None else (lambda b, i: (mod_row, 0, 0))
    in_specs = [
        pl.BlockSpec((None, tm, d), lambda b, i: (b, i, 0)),
        pl.BlockSpec((None, 6, d), row),
        _const_spec((1, d)),
        _const_spec(w.shape),
    ]
    args = [x, mod, g, w]
    if rope:
        in_specs += [pl.BlockSpec((tm, LANES), lambda b, i: (i, 0))] * 4
        args += list(tables)
    rows = lambda n: (jax.ShapeDtypeStruct((bsz, t, n), BF16), pl.BlockSpec((None, tm, n), lambda b, i: (b, i, 0)))
    keys_t = (jax.ShapeDtypeStruct((bsz, t // ATTN_BLOCK, nk, ATTN_BLOCK), BF16),
              pl.BlockSpec((None, tm // ATTN_BLOCK, nk, ATTN_BLOCK), lambda b, i: (b, i, 0, 0)))
    outs = [rows(nq), keys_t, rows(nk), rows(nu)]
    return pl.pallas_call(
        functools.partial(_in0_kernel, rope=rope),
        out_shape=[o[0] for o in outs],
        grid=(bsz, t // tm),
        in_specs=in_specs,
        out_specs=[o[1] for o in outs],
        compiler_params=_params("parallel", "parallel"),
        name="in0_rope" if rope else "in0_ctx",
    )(*args)


def _mix0_kernel(*refs, band, t_len, tq):
    if band:
        (sink_ref, x_ref, q_ref, k_ref, v_ref, kc_ref, vc_ref, u_ref, pw_ref, ps_ref, wo_ref, g_ref, mod_ref,
         o_ref, cat_ref) = refs
    else:
        (sink_ref, x_ref, q_ref, kc_ref, vc_ref, u_ref, pw_ref, ps_ref, wo_ref, g_ref, mod_ref,
         o_ref, cat_ref) = refs
    blk = ATTN_BLOCK
    nb = t_len // blk
    nsub = tq // blk
    tstep = pl.program_id(1)
    n_ctx = vc_ref.shape[0]
    n_keys = (3 * blk if band else 0) + n_ctx

    qrow = jnp.bitwise_and(lax.broadcasted_iota(jnp.int32, (2 * blk, blk), 0), blk - 1)
    kcol = lax.broadcasted_iota(jnp.int32, (2 * blk, blk), 1)
    tri_prev = kcol >= qrow
    tri_next = kcol <= qrow
    top_rows = lax.broadcasted_iota(jnp.int32, (2 * blk, 1), 0) < blk
    dim_lo = lax.broadcasted_iota(jnp.int32, (LANES, n_keys), 0) < HEAD_DIM
    lane_lo = lax.broadcasted_iota(jnp.int32, (n_keys, LANES), 1) < HEAD_DIM
    lane_o = lax.broadcasted_iota(jnp.int32, (2 * blk, LANES), 1) < HEAD_DIM
    zero = jnp.zeros((), BF16)
    ones_lo = jnp.where(lane_lo, 1.0, 0.0).astype(BF16)
    ones_hi = jnp.where(lane_lo, 0.0, 1.0).astype(BF16)
    row3 = lax.broadcasted_iota(jnp.int32, (blk, 3 * blk), 0)
    col3 = lax.broadcasted_iota(jnp.int32, (blk, 3 * blk), 1) - blk
    pool_band = [((col3 >= row3 - w // 2) & (col3 <= row3 + w // 2 - 1)).astype(BF16) for w in POOL_WINDOWS]
    row_pos = lax.broadcasted_iota(jnp.int32, (blk, LANES), 0)

    for j in range(nsub):
        n = tstep * nsub + j
        pblk = jnp.maximum(n - 1, 0)
        nblk = jnp.minimum(n + 1, nb - 1)
        r0 = pl.multiple_of(n * blk, blk)
        ps = pl.multiple_of(pblk * blk, blk)
        ns = pl.multiple_of(nblk * blk, blk)
        has_prev = n > 0
        has_next = n < nb - 1
        rows = slice(j * blk, (j + 1) * blk)

        if band:
            vb = jnp.concatenate([v_ref[pl.ds(ps, blk), :], v_ref[pl.ds(r0, blk), :], v_ref[pl.ds(ns, blk), :],
                                  vc_ref[...]], axis=0)
        else:
            vb = vc_ref[...]
        for hk in range(ATTN_KV_HEADS):
            hd = slice(hk * LANES, (hk + 1) * LANES)
            kt = [kc_ref[i, hd, :] for i in range(kc_ref.shape[0])]
            if band:
                kt = [k_ref[pblk, hd, :], k_ref[n, hd, :], k_ref[nblk, hd, :]] + kt
            kt = jnp.concatenate(kt, axis=1)
            vv = vb[:, hd]
            v_aug = jnp.concatenate(
                [jnp.concatenate([jnp.where(lane_lo, vv, zero), ones_lo], axis=1),
                 jnp.concatenate([jnp.where(lane_lo, zero, vv), ones_hi], axis=1)], axis=0)
            c0, c1 = 2 * hk, 2 * hk + 1
            q2 = jnp.concatenate([q_ref[rows, c0 * LANES:(c0 + 1) * LANES],
                                  q_ref[rows, c1 * LANES:(c1 + 1) * LANES]], axis=0)
            probs, esink = [], []
            for half in range(2):
                s = _dot(q2, jnp.where(dim_lo, kt, zero) if half == 0 else jnp.where(dim_lo, zero, kt))
                if band:
                    s = jnp.concatenate([jnp.where(tri_prev & has_prev, s[:, :blk], NEG), s[:, blk:2 * blk],
                                         jnp.where(tri_next & has_next, s[:, 2 * blk:3 * blk], NEG),
                                         s[:, 3 * blk:]], axis=1)
                snk = jnp.where(top_rows, sink_ref[4 * hk + half], sink_ref[4 * hk + 2 + half])
                m = jnp.maximum(jnp.max(s, axis=-1, keepdims=True), snk)
                probs.append(jnp.exp(s - m).astype(BF16))
                esink.append(jnp.exp(snk - m))
            o2 = _dot(jnp.concatenate(probs, axis=1), v_aug)
            o = o2[:, :LANES] / (o2[:, LANES:] + jnp.where(lane_o, esink[0], esink[1]))
            cat_ref[rows, c0 * LANES:(c0 + 1) * LANES] = o[:blk].astype(BF16)
            cat_ref[rows, c1 * LANES:(c1 + 1) * LANES] = o[blk:].astype(BF16)

        uo = u_ref[pl.ds(r0, blk), :]
        up = jnp.where(has_prev, u_ref[pl.ds(ps, blk), :], zero)
        un = jnp.where(has_next, u_ref[pl.ds(ns, blk), :], zero)
        ub = jnp.concatenate([up, uo, un], axis=0)
        pos = r0 + row_pos
        for gi, w in enumerate(POOL_WINDOWS):
            cs = slice(gi * POOL_CH, (gi + 1) * POOL_CH)
            sums = _dot(pool_band[gi], ub[:, cs])
            cnt = jnp.minimum(pos + w // 2, t_len) - jnp.maximum(pos - w // 2, 0)
            dlt = sums / cnt.astype(F32) - uo[:, cs].astype(F32)
            y = _dot(dlt.astype(BF16), pw_ref[gi]) * ps_ref[:, cs]
            off = ATTN_Q_HEADS * HEAD_DIM + gi * POOL_CH
            cat_ref[rows, off:off + POOL_CH] = y.astype(BF16)

    y = _dot(cat_ref[...], wo_ref[...])
    o_ref[...] = x_ref[...] + mod_ref[2:3, :] * _rms(y, g_ref[...])


def _mix0(x, q, k4, v4, kc4, vc4, u, sink, pool_w, pool_scale, w_out, g, mod, mod_row):
    bsz, t, d = x.shape
    band = k4 is not None
    tq = min(t, 512)
    n_ctx = vc4.shape[1]
    row = (lambda b, i: (b, 0, 0)) if mod_row is None else (lambda b, i: (mod_row, 0, 0))
    full = lambda n, w: pl.BlockSpec((None, n, w), lambda b, i: (b, 0, 0))
    keys_t = lambda a: pl.BlockSpec((None,) + a.shape[1:], lambda b, i: (b, 0, 0, 0))
    in_specs = [
        pl.BlockSpec(memory_space=pltpu.SMEM),
        pl.BlockSpec((None, tq, d), lambda b, i: (b, i, 0)),
        pl.BlockSpec((None, tq, q.shape[2]), lambda b, i: (b, i, 0)),
    ]
    args = [sink, x, q]
    if band:
        in_specs += [keys_t(k4), full(t, v4.shape[2])]
        args += [k4, v4]
    in_specs += [keys_t(kc4), full(n_ctx, vc4.shape[2]), full(t, u.shape[2]),
                 _const_spec(pool_w.shape), _const_spec(pool_scale.shape), _const_spec(w_out.shape),
                 _const_spec(g.shape), pl.BlockSpec((None, 6, d), row)]
    args += [kc4, vc4, u, pool_w, pool_scale, w_out, g, mod]
    return pl.pallas_call(
        functools.partial(_mix0_kernel, band=band, t_len=t, tq=tq),
        out_shape=jax.ShapeDtypeStruct((bsz, t, d), F32),
        grid=(bsz, t // tq),
        in_specs=in_specs,
        out_specs=pl.BlockSpec((None, tq, d), lambda b, i: (b, i, 0)),
        scratch_shapes=[pltpu.VMEM((tq, w_out.shape[0]), BF16)],
        compiler_params=_params("parallel", "parallel"),
        name="mix0_band" if band else "mix0_ctx",
    )(*args)


HALO = SUBLANES


def _halo_specs(tm, t, d):
    nblk = tm // HALO
    last = t // HALO - 1
    return [
        pl.BlockSpec((None, tm, d), lambda b, i: (b, i, 0)),
        pl.BlockSpec((None, HALO, d), lambda b, i: (b, jnp.maximum(i * nblk - 1, 0), 0)),
        pl.BlockSpec((None, HALO, d), lambda b, i: (b, jnp.minimum((i + 1) * nblk, last), 0)),
    ]


def _conv3(u, cw, cb, tm):
    return (cw[0:1, :] * u[HALO - 1:HALO - 1 + tm, :] + cw[1:2, :] * u[HALO:HALO + tm, :]
            + cw[2:3, :] * u[HALO + 1:HALO + 1 + tm, :] + cb)


def _ffn_kernel(xm_ref, xp_ref, xn_ref, mod_ref, gpre_ref, gpost_ref, wup_ref, cw_ref, cb_ref, wdn_ref,
                o_ref, hb_ref, acc_ref, u0_ref, u1_ref, *, tm, nt):
    i = pl.program_id(1)
    shift, scale, gate = mod_ref[3:4, :], mod_ref[4:5, :], mod_ref[5:6, :]
    gpre = gpre_ref[...]

    def pre(xv):
        return _rms(xv, gpre) * (1.0 + scale) + shift

    hp = jnp.where(i > 0, pre(xp_ref[...]), 0.0)
    hn = jnp.where(i < nt - 1, pre(xn_ref[...]), 0.0)
    hb_ref[...] = jnp.concatenate([hp, pre(xm_ref[...]), hn], axis=0).astype(BF16)
    acc_ref[...] = jnp.zeros_like(acc_ref)
    nchunk = wdn_ref.shape[0]
    ncol = wup_ref.shape[2] // LANES
    half = ncol // 2
    u_refs = (u0_ref, u1_ref)

    def up(c, s):
        u = _dot(hb_ref[...], wup_ref[c])
        for j in range(ncol):
            u_refs[s][j] = u[:, j * LANES:(j + 1) * LANES]

    def conv_act_down(c, s):
        cw, cb, ur = cw_ref[c], cb_ref[c], u_refs[s]

        def conv(j):
            cs = slice(j * LANES, (j + 1) * LANES)
            return (cw[0:1, cs] * ur[j, pl.ds(HALO - 1, tm), :] + cw[1:2, cs] * ur[j, pl.ds(HALO, tm), :]
                    + cw[2:3, cs] * ur[j, pl.ds(HALO + 1, tm), :] + cb[:, cs])

        acts = []
        for j in range(half):
            ag, av = conv(j), conv(half + j)
            acts.append((ag * _sigmoid(ag) * av).astype(BF16))
        acc_ref[...] += _dot(jnp.concatenate(acts, axis=1), wdn_ref[c])

    up(0, 0)
    for c in range(nchunk):
        if c + 1 < nchunk:
            up(c + 1, (c + 1) % 2)
        conv_act_down(c, c % 2)
    o_ref[...] = xm_ref[...] + gate * _rms(acc_ref[...], gpost_ref[...])


def _ffn(x, mod, mod_row, gpre, gpost, wup3, cw3, cb3, wdn3):
    bsz, t, d = x.shape
    tm = min(t, 512)
    nt = t // tm
    ck = wup3.shape[2]
    row = (lambda b, i: (b, 0, 0)) if mod_row is None else (lambda b, i: (mod_row, 0, 0))
    in_specs = _halo_specs(tm, t, d) + [
        pl.BlockSpec((None, 6, d), row), _const_spec(gpre.shape), _const_spec(gpost.shape),
        _const_spec(wup3.shape), _const_spec(cw3.shape), _const_spec(cb3.shape), _const_spec(wdn3.shape)]
    u_buf = pltpu.VMEM((ck // LANES, tm + 2 * HALO, LANES), F32)
    return pl.pallas_call(
        functools.partial(_ffn_kernel, tm=tm, nt=nt),
        out_shape=jax.ShapeDtypeStruct((bsz, t, d), F32),
        grid=(bsz, nt),
        in_specs=in_specs,
        out_specs=pl.BlockSpec((None, tm, d), lambda b, i: (b, i, 0)),
        scratch_shapes=[pltpu.VMEM((tm + 2 * HALO, d), BF16), pltpu.VMEM((tm, d), F32), u_buf, u_buf],
        compiler_params=_params("parallel", "parallel"),
        name="ffn",
    )(x, x, x, mod, gpre, gpost, wup3, cw3, cb3, wdn3)


def _in1_kernel(*refs, tm, nt, full):
    if full:
        (xm_ref, xp_ref, xn_ref, mod_ref, g_ref, wu_ref, wr_ref, gb_ref, cw_ref, cb_ref, qw_ref, kw_ref,
         uc_ref, og_ref, v_ref, q_ref, k_ref, gt_ref) = refs
    else:
        (xm_ref, xp_ref, xn_ref, mod_ref, g_ref, wu_ref, wr_ref, gb_ref, cw_ref, cb_ref, kw_ref,
         v_ref, k_ref, gt_ref) = refs
    i = pl.program_id(1)
    shift, scale = mod_ref[0:1, :], mod_ref[1:2, :]
    g = g_ref[...]

    def pre(xv):
        return _rms(xv, g) * (1.0 + scale) + shift

    hm = pre(xm_ref[...])
    hp = jnp.where(i > 0, pre(xp_ref[...]), 0.0)
    hn = jnp.where(i < nt - 1, pre(xn_ref[...]), 0.0)
    hext = jnp.concatenate([hp, hm, hn], axis=0).astype(BF16)
    u = _dot(hext, wu_ref[...])
    uc = _conv3(u, cw_ref[...], cb_ref[...], tm)
    uc = uc * _sigmoid(uc)
    ucb = uc.astype(BF16)
    rest = _dot(hm.astype(BF16), wr_ref[...])
    dv = v_ref.shape[1]
    v_ref[...] = rest[:, :dv].astype(BF16)
    if full:
        og_ref[...] = rest[:, dv:2 * dv].astype(BF16)
        uc_ref[...] = ucb
        gt_ref[...] = rest[:, 2 * dv:] + gb_ref[...]
    else:
        gt_ref[...] = rest[:, dv:] + gb_ref[...]
    dh = MLSTM_HEAD_DIM
    for hh in range(MLSTM_HEADS):
        uh = ucb[:, hh * dh:(hh + 1) * dh]
        k_ref[:, hh * dh:(hh + 1) * dh] = _dot(uh, kw_ref[hh]).astype(BF16)
        if full:
            q_ref[:, hh * dh:(hh + 1) * dh] = _dot(uh, qw_ref[hh]).astype(BF16)


def _in1(x, mod, mod_row, g, wu, wr, gb, cw, cb, qw, kw, full):
    bsz, t, d = x.shape
    tm = min(t, 512)
    nt = t // tm
    dw = wu.shape[1]
    row = (lambda b, i: (b, 0, 0)) if mod_row is None else (lambda b, i: (mod_row, 0, 0))
    in_specs = _halo_specs(tm, t, d) + [pl.BlockSpec((None, 6, d), row)]
    consts = [g, wu, wr, gb, cw, cb] + ([qw] if full else []) + [kw]
    in_specs += [_const_spec(a.shape) for a in consts]
    act = lambda: jax.ShapeDtypeStruct((bsz, t, dw), BF16)
    gates = jax.ShapeDtypeStruct((bsz, t, LANES), F32)
    out_shape = [act(), act(), act(), act(), act(), gates] if full else [act(), act(), gates]
    out_specs = [pl.BlockSpec((None, tm, s.shape[2]), lambda b, i: (b, i, 0)) for s in out_shape]
    return pl.pallas_call(
        functools.partial(_in1_kernel, tm=tm, nt=nt, full=full),
        out_shape=out_shape,
        grid=(bsz, nt),
        in_specs=in_specs,
        out_specs=out_specs,
        compiler_params=_params("parallel", "parallel"),
        name="in1_full" if full else "in1_ctx",
    )(x, x, x, mod, *consts)


def _log_sigmoid(x):
    return jnp.minimum(x, 0.0) - jnp.log1p(jnp.exp(-jnp.abs(x)))


def _scan_lanes(x, op, fill, reverse):
    n = x.shape[1]
    lane = lax.broadcasted_iota(jnp.int32, x.shape, 1)
    d = 1
    while d < n:
        if reverse:
            shifted = jnp.where(lane < n - d, pltpu.roll(x, n - d, 1), fill)
        else:
            shifted = jnp.where(lane >= d, pltpu.roll(x, d, 1), fill)
        x = op(x, shifted)
        d *= 2
    return x


def _mlstm_kernel(q_ref, k_ref, v_ref, kc_ref, vc_ref, gl_ref, gc_ref, og_ref, uc_ref, ng_ref, skip_ref,
                  y_ref, cf_ref, cb_ref, *, nc):
    ln = MLSTM_CHUNK
    dh = q_ref.shape[1]
    nr = nc + 1
    gates = [jnp.concatenate([gc_ref[gi], gl_ref[gi]], axis=0) for gi in range(4)]

    def direction(ig, fpre, reverse):
        b = _scan_lanes(_log_sigmoid(fpre), jnp.add, 0.0, reverse)
        a = ig - b
        cmax = _scan_lanes(a, jnp.maximum, NEG, reverse)
        e = 0 if reverse else ln - 1
        b_end, a_max = b[:, e:e + 1], cmax[:, e:e + 1]
        order = [0] + (list(range(nc, 0, -1)) if reverse else list(range(1, nr)))
        m_in = [None] * nr
        m = jnp.zeros((1, 1), F32)
        for r in order:
            m_in[r] = m
            m = b_end[r:r + 1] + jnp.maximum(m, a_max[r:r + 1])
        m_in = jnp.concatenate(m_in, axis=0)
        big_m = jnp.maximum(m_in, cmax)
        m_end = jnp.maximum(m_in, a_max)
        return dict(a=a, M=big_m, mt=b + big_m, m_in=m_in, wts=jnp.exp(a - m_end), decay=jnp.exp(m_in - m_end))

    fw = direction(gates[0], gates[1], False)
    bw = direction(gates[2], gates[3], True)

    def lane_bcast(row):
        return jnp.transpose(jnp.broadcast_to(row, (LANES, ln)))

    def wide(a):
        return jnp.concatenate([a] * (dh // LANES), axis=1)

    ones = jnp.ones((ln, LANES), BF16)

    def chunk_kv(r):
        if r == 0:
            return kc_ref[...], vc_ref[...]
        return k_ref[(r - 1) * ln:r * ln, :], v_ref[(r - 1) * ln:r * ln, :]

    contrib_f, contrib_b = [], []
    for r in range(nr):
        kk, vv = chunk_kv(r)
        kt = jnp.transpose(kk.astype(F32))
        v_aug = jnp.concatenate([vv, ones], axis=1)
        contrib_f.append(_dot((kt * fw["wts"][r:r + 1]).astype(BF16), v_aug))
        contrib_b.append(_dot((kt * bw["wts"][r:r + 1]).astype(BF16), v_aug))

    st = contrib_f[0]
    for c in range(nc):
        cf_ref[c] = st.astype(BF16)
        if c + 1 < nc:
            st = fw["decay"][c + 1:c + 2] * st + contrib_f[c + 1]
    st = contrib_b[0]
    for c in range(nc - 1, -1, -1):
        cb_ref[c] = st.astype(BF16)
        if c > 0:
            st = bw["decay"][c + 1:c + 2] * st + contrib_b[c + 1]

    ti = lax.broadcasted_iota(jnp.int32, (ln, ln), 0)
    sj = lax.broadcasted_iota(jnp.int32, (ln, ln), 1)
    ng = ng_ref[...]
    skip = skip_ref[...]
    for c in range(nc):
        r = c + 1
        qc = q_ref[c * ln:(c + 1) * ln, :]
        kk, vv = chunk_kv(r)
        s = _dot_nt(qc, kk)
        qf = qc.astype(F32)
        lhs = []
        p_sum = None
        for dirn, causal, st_ref in ((fw, sj <= ti, cf_ref), (bw, sj >= ti, cb_ref)):
            big_m = lane_bcast(dirn["M"][r:r + 1])
            p = s * jnp.exp(jnp.where(causal, dirn["a"][r:r + 1] - wide(big_m), NEG))
            iw = jnp.exp(dirn["m_in"][r:r + 1] - big_m)
            den = _dot(p.astype(BF16), ones) + iw * _dot(qc, st_ref[c, :, dh:])
            rinv = 1.0 / jnp.maximum(jnp.abs(den), jnp.exp(-lane_bcast(dirn["mt"][r:r + 1])))
            p_sum = p * wide(rinv) if p_sum is None else p_sum + p * wide(rinv)
            lhs.append((qf * wide(iw * rinv)).astype(BF16))
        lhs = jnp.concatenate([p_sum.astype(BF16)] + lhs, axis=1)
        rhs = jnp.concatenate([vv, cf_ref[c, :, :dh], cb_ref[c, :, :dh]], axis=0)
        hsum = _dot(lhs, rhs)
        rows = slice(c * ln, (c + 1) * ln)
        y = _sigmoid(og_ref[rows, :].astype(F32)) * (_rms(hsum, ng) + skip * uc_ref[rows, :].astype(F32))
        y_ref[rows, :] = y.astype(BF16)


def _mlstm(q, k, v, kc, vc, gl, gc, og, uc, ng, skip):
    bsz, t, wd = q.shape
    dh = MLSTM_HEAD_DIM
    nh = wd // dh
    nc = t // MLSTM_CHUNK
    n_ctx = kc.shape[1]
    assert n_ctx == MLSTM_CHUNK and gl.shape == (bsz, nh, 4, nc, MLSTM_CHUNK)
    seq = lambda n: pl.BlockSpec((None, n, dh), lambda b, h: (b, 0, h))
    head_vec = pl.BlockSpec((1, dh), lambda b, h: (0, h))
    state = pltpu.VMEM((nc, dh, dh + LANES), BF16)
    return pl.pallas_call(
        functools.partial(_mlstm_kernel, nc=nc),
        out_shape=jax.ShapeDtypeStruct((bsz, t, wd), BF16),
        grid=(bsz, nh),
        in_specs=[seq(t), seq(t), seq(t), seq(n_ctx), seq(n_ctx),
                  pl.BlockSpec((None, None, 4, nc, MLSTM_CHUNK), lambda b, h: (b, h, 0, 0, 0)),
                  pl.BlockSpec((None, None, 4, 1, MLSTM_CHUNK), lambda b, h: (b, h, 0, 0, 0)),
                  seq(t), seq(t), head_vec, head_vec],
        out_specs=seq(t),
        scratch_shapes=[state, state],
        compiler_params=_params("parallel", "parallel"),
        name="mlstm",
    )(q, k, v, kc, vc, gl, gc, og, uc, ng, skip)


def _out1_kernel(x_ref, y_ref, wo_ref, g_ref, mod_ref, o_ref):
    y = _dot(y_ref[...], wo_ref[...])
    o_ref[...] = x_ref[...] + mod_ref[2:3, :] * _rms(y, g_ref[...])


def _out1(x, y, w_out, g, mod):
    bsz, t, d = x.shape
    tm = min(t, 512)
    wd = y.shape[2]
    tile = lambda n: pl.BlockSpec((None, tm, n), lambda b, i: (b, i, 0))
    return pl.pallas_call(
        _out1_kernel,
        out_shape=jax.ShapeDtypeStruct((bsz, t, d), F32),
        grid=(bsz, t // tm),
        in_specs=[tile(d), tile(wd), _const_spec(w_out.shape), _const_spec(g.shape),
                  pl.BlockSpec((None, 6, d), lambda b, i: (b, 0, 0))],
        out_specs=tile(d),
        compiler_params=_params("parallel", "parallel"),
        name="out1",
    )(x, y, w_out, g, mod)


def _rope_tables(t_len, scale):
    rows = t_len // GRID_W
    row = jnp.repeat(jnp.arange(rows, dtype=F32), GRID_W)
    col = jnp.tile(jnp.arange(GRID_W, dtype=F32), rows)
    n_freq = HEAD_DIM // 4
    inv = ROPE_BASE ** (-jnp.arange(n_freq, dtype=F32) / n_freq)
    ar, ac = row[:, None] * inv, col[:, None] * inv
    cos = jnp.concatenate([jnp.cos(ar), jnp.cos(ar), jnp.cos(ac), jnp.cos(ac)], axis=-1)
    sin = jnp.concatenate([-jnp.sin(ar), jnp.sin(ar), -jnp.sin(ac), jnp.sin(ac)], axis=-1)
    reps = LANES // HEAD_DIM
    return jnp.tile(cos, (1, reps)) * scale, jnp.tile(sin, (1, reps)) * scale


def _ffn_weights(w_up, conv_w, conv_b, w_down, ck=256):
    n2 = w_up.shape[1]
    nch = n2 // (2 * ck)
    pair = lambda a: jnp.concatenate([a[..., :n2 // 2].reshape(a.shape[:-1] + (nch, ck)),
                                      a[..., n2 // 2:].reshape(a.shape[:-1] + (nch, ck))], axis=-1)
    wup3 = pair(w_up.astype(BF16)).transpose(1, 0, 2)
    cw3 = pair(conv_w).transpose(1, 0, 2)
    cb3 = pair(conv_b)[:, None, :]
    wdn3 = w_down.astype(BF16).reshape(nch, ck, w_down.shape[1])
    return wup3, cw3, cb3, wdn3


def kernel(x, c, ctx, c_ctx, mod_w, mod_b, norm_g, attn_in_w, attn_sink, pool_w, pool_scale, attn_out_w,
           rec_in_w, rec_gate_b, rec_conv_w, rec_conv_b, rec_q_w, rec_k_w, rec_norm_g, rec_skip, rec_out_w,
           ffn_up_w, ffn_conv_w, ffn_conv_b, ffn_down_w):
    bsz, t, d = x.shape
    n_ctx = ctx.shape[1]
    ctx_row = bsz

    pad_rows = (-(bsz + 1)) % 16
    c_all = jnp.concatenate([c, c_ctx[None, :], jnp.zeros((pad_rows, d), F32)], axis=0)
    mod = _adaln(c_all, mod_w, mod_b)
    mod = mod.reshape(mod.shape[0], mod.shape[1], 6, d)
    row2 = lambda a: a.reshape(1, -1)

    w = attn_in_w[0]
    aw, kvw = ATTN_Q_HEADS * HEAD_DIM, HEAD_DIM
    dup = lambda m: jnp.concatenate([m[:, :kvw], m[:, :kvw], m[:, kvw:], m[:, kvw:]], axis=1)
    w_in0 = jnp.concatenate([w[:, :aw], dup(w[:, aw:aw + 2 * kvw]), dup(w[:, aw + 2 * kvw:aw + 4 * kvw]),
                             w[:, aw + 4 * kvw:]], axis=1).astype(BF16)
    qscale = HEAD_DIM ** -0.5
    tables = _rope_tables(t, qscale) + _rope_tables(t, 1.0)
    g00 = row2(norm_g[0, 0])
    q, k4, v4, u = _in0(x, mod[0], None, g00, w_in0, tables)
    qc, kc4, vc4, ucx = _in0(ctx, mod[0], ctx_row, g00, w_in0, None)
    pw = pool_w[0].astype(BF16)
    psc = row2(pool_scale[0])
    wo0 = attn_out_w[0].astype(BF16)
    g01 = row2(norm_g[0, 1])
    sink = attn_sink[0]
    x1 = _mix0(x, q, k4, v4, kc4, vc4, u, sink, pw, psc, wo0, g01, mod[0], None)
    c1 = _mix0(ctx, qc, None, None, kc4, vc4, ucx, sink, pw, psc, wo0, g01, mod[0], ctx_row)
    f0 = _ffn_weights(ffn_up_w[0], ffn_conv_w[0], ffn_conv_b[0], ffn_down_w[0])
    g02, g03 = row2(norm_g[0, 2]), row2(norm_g[0, 3])
    x2 = _ffn(x1, mod[0], None, g02, g03, *f0)
    c2 = _ffn(c1, mod[0], ctx_row, g02, g03, *f0)

    w = rec_in_w[0]
    wd = MLSTM_HEADS * MLSTM_HEAD_DIM
    ng = 4 * MLSTM_HEADS
    gcol = w[:, 3 * wd:].reshape(d, 4, MLSTM_HEADS).transpose(0, 2, 1).reshape(d, ng)
    gcol = jnp.concatenate([gcol, jnp.zeros((d, LANES - ng), F32)], axis=1)
    gbias = jnp.concatenate([rec_gate_b[0].T.reshape(1, ng), jnp.zeros((1, LANES - ng), F32)], axis=1)
    wu = w[:, :wd].astype(BF16)
    wr_full = jnp.concatenate([w[:, wd:3 * wd], gcol], axis=1).astype(BF16)
    wr_ctx = jnp.concatenate([w[:, wd:2 * wd], gcol], axis=1).astype(BF16)
    cw, cb = rec_conv_w[0], row2(rec_conv_b[0])
    qw = rec_q_w[0].astype(BF16)
    kw = (rec_k_w[0] * (MLSTM_HEAD_DIM ** -0.5)).astype(BF16)
    g10 = row2(norm_g[1, 0])
    uc, og, v, qm, km, gts = _in1(x2, mod[1], None, g10, wu, wr_full, gbias, cw, cb, qw, kw, True)
    vcx, kcx, gtc = _in1(c2, mod[1], ctx_row, g10, wu, wr_ctx, gbias, cw, cb, None, kw, False)

    def gate_rows(gt, n):
        gt = gt[:, :, :ng].reshape(bsz, n // MLSTM_CHUNK, MLSTM_CHUNK, MLSTM_HEADS, 4)
        return gt.transpose(0, 3, 4, 1, 2)

    y = _mlstm(qm, km, v, kcx, vcx, gate_rows(gts, t), gate_rows(gtc, n_ctx), og, uc,
               row2(rec_norm_g[0]), row2(rec_skip[0]))
    x3 = _out1(x2, y, rec_out_w[0].astype(BF16), row2(norm_g[1, 1]), mod[1])
    f1 = _ffn_weights(ffn_up_w[1], ffn_conv_w[1], ffn_conv_b[1], ffn_down_w[1])
    return _ffn(x3, mod[1], None, row2(norm_g[1, 2]), row2(norm_g[1, 3]), *f1)
```

```python
import functools

import jax
import jax.numpy as jnp
from jax import lax
from jax.experimental import pallas as pl
from jax.experimental.pallas import tpu as pltpu

F32 = jnp.float32
BF16 = jnp.bfloat16

GRID_W = 64
HEAD_DIM = 64
ATTN_Q_HEADS = 8
ATTN_KV_HEADS = 2
ATTN_BLOCK = 128
ROPE_BASE = 10000.0
POOL_WINDOWS = (2, 4, 8, 16)
POOL_CH = 128
MLSTM_HEADS = 4
MLSTM_HEAD_DIM = 256
MLSTM_CHUNK = 256
EPS = 1e-6
NEG = -1e30

LANES = 128
SUBLANES = 8
VMEM_LIMIT = 56 * 1024 * 1024


def _params(*sem):
    return pltpu.CompilerParams(dimension_semantics=sem, vmem_limit_bytes=VMEM_LIMIT)


def _rms(x, g):
    return x * lax.rsqrt(jnp.mean(x * x, axis=-1, keepdims=True) + EPS) * g


def _sigmoid(x):
    return 1.0 / (1.0 + jnp.exp(-x))


def _dot(a, b):
    return jnp.dot(a, b, preferred_element_type=F32)


def _dot_nt(a, b):
    return lax.dot_general(a, b, (((1,), (1,)), ((), ())), preferred_element_type=F32)


def _dot_tn(a, b):
    return lax.dot_general(a, b, (((0,), (0,)), ((), ())), preferred_element_type=F32)


def _const_spec(shape):
    nd = len(shape)
    return pl.BlockSpec(shape, lambda *_: (0,) * nd)


def _adaln_kernel(c_ref, w_ref, b_ref, o_ref):
    c = c_ref[...]
    s = c * _sigmoid(c)
    o_ref[...] = _dot(s.astype(BF16), w_ref[...].astype(BF16)) + b_ref[...]


def _adaln(c_all, mod_w, mod_b):
    depth, d, n6 = mod_w.shape
    rows = c_all.shape[0]
    tn = 1536
    return pl.pallas_call(
        _adaln_kernel,
        out_shape=jax.ShapeDtypeStruct((depth, rows, n6), F32),
        grid=(depth, n6 // tn),
        in_specs=[
            pl.BlockSpec((rows, d), lambda l, j: (0, 0)),
            pl.BlockSpec((None, d, tn), lambda l, j: (l, 0, j)),
            pl.BlockSpec((None, 1, tn), lambda l, j: (l, 0, j)),
        ],
        out_specs=pl.BlockSpec((None, rows, tn), lambda l, j: (l, 0, j)),
        compiler_params=_params("arbitrary", "arbitrary"),
        name="adaln",
    )(c_all, mod_w, mod_b.reshape(depth, 1, n6))


def _in0_kernel(*refs, rope):
    if rope:
        x_ref, mod_ref, g_ref, w_ref, cq_ref, sq_ref, ck_ref, sk_ref, q_ref, k_ref, v_ref, u_ref = refs
    else:
        x_ref, mod_ref, g_ref, w_ref, q_ref, k_ref, v_ref, u_ref = refs
    x = x_ref[...]
    h = _rms(x, g_ref[...]) * (1.0 + mod_ref[1:2, :]) + mod_ref[0:1, :]
    y = _dot(h.astype(BF16), w_ref[...])
    nq = q_ref.shape[1]
    nkv = ATTN_KV_HEADS * HEAD_DIM
    tm = x.shape[0]
    lane = lax.broadcasted_iota(jnp.int32, (tm, LANES), 1)
    first = lane < HEAD_DIM

    def dup(a):
        sw = pltpu.roll(a, HEAD_DIM, 1)
        return jnp.concatenate([jnp.where(first, a, sw), jnp.where(first, sw, a)], axis=1)

    def put_keys(kf):
        for jb in range(tm // ATTN_BLOCK):
            k_ref[jb] = jnp.transpose(kf[jb * ATTN_BLOCK:(jb + 1) * ATTN_BLOCK, :]).astype(BF16)

    if rope:
        even = (lane // 16) % 2 == 0

        def rot(a, c, s):
            sw = jnp.where(even, pltpu.roll(a, LANES - 16, 1), pltpu.roll(a, 16, 1))
            return a * c + sw * s

        cq, sq, ck, sk = cq_ref[...], sq_ref[...], ck_ref[...], sk_ref[...]
        for j in range(nq // LANES):
            q_ref[:, j * LANES:(j + 1) * LANES] = rot(y[:, j * LANES:(j + 1) * LANES], cq, sq).astype(BF16)
        put_keys(dup(rot(y[:, nq:nq + nkv], ck, sk)))
    else:
        q_ref[...] = (y[:, :nq] * (HEAD_DIM ** -0.5)).astype(BF16)
        put_keys(dup(y[:, nq:nq + nkv]))
    v_ref[...] = dup(y[:, nq + nkv:nq + 2 * nkv]).astype(BF16)
    u_ref[...] = y[:, nq + 2 * nkv:].astype(BF16)


def _in0(x, mod, mod_row, g, w, tables):
    bsz, t, d = x.shape
    tm = min(t, LIGHT_ROWS)
    rope = tables is not None
    nq, nk, nu = 512, 256, 512
    row = (lambda b, i: (b, 0, 0)) if mod_row is None else (lambda b, i: (mod_row, 0, 0))
    in_specs = [
        pl.BlockSpec((None, tm, d), lambda b, i: (b, i, 0)),
        pl.BlockSpec((None, 6, d), row),
        _const_spec((1, d)),
        _const_spec(w.shape),
    ]
    args = [x, mod, g, w]
    if rope:
        in_specs += [pl.BlockSpec((tm, LANES), lambda b, i: (i, 0))] * 4
        args += list(tables)
    rows = lambda n: (jax.ShapeDtypeStruct((bsz, t, n), BF16), pl.BlockSpec((None, tm, n), lambda b, i: (b, i, 0)))
    keys_t = (jax.ShapeDtypeStruct((bsz, t // ATTN_BLOCK, nk, ATTN_BLOCK), BF16),
              pl.BlockSpec((None, tm // ATTN_BLOCK, nk, ATTN_BLOCK), lambda b, i: (b, i, 0, 0)))
    outs = [rows(nq), keys_t, rows(nk), rows(nu)]
    return pl.pallas_call(
        functools.partial(_in0_kernel, rope=rope),
        out_shape=[o[0] for o in outs],
        grid=(bsz, t // tm),
        in_specs=in_specs,
        out_specs=[o[1] for o in outs],
        compiler_params=_params("parallel", "parallel"),
        name="in0_rope" if rope else "in0_ctx",
    )(*args)


def _mix0_kernel(*refs, band, t_len, tq):
    if band:
        (sink_ref, x_ref, q_ref, k_ref, v_ref, kc_ref, vc_ref, u_ref, pw_ref, ps_ref, wo_ref, g_ref, mod_ref,
         o_ref, cat_ref) = refs
    else:
        (sink_ref, x_ref, q_ref, kc_ref, vc_ref, u_ref, pw_ref, ps_ref, wo_ref, g_ref, mod_ref,
         o_ref, cat_ref) = refs
    blk = ATTN_BLOCK
    nb = t_len // blk
    nsub = tq // blk
    tstep = pl.program_id(1)
    n_ctx = vc_ref.shape[0]
    n_keys = (3 * blk if band else 0) + n_ctx

    qrow = jnp.bitwise_and(lax.broadcasted_iota(jnp.int32, (2 * blk, blk), 0), blk - 1)
    kcol = lax.broadcasted_iota(jnp.int32, (2 * blk, blk), 1)
    tri_prev = kcol >= qrow
    tri_next = kcol <= qrow
    top_rows = lax.broadcasted_iota(jnp.int32, (2 * blk, 1), 0) < blk
    dim_lo = lax.broadcasted_iota(jnp.int32, (LANES, n_keys), 0) < HEAD_DIM
    lane_lo = lax.broadcasted_iota(jnp.int32, (n_keys, LANES), 1) < HEAD_DIM
    lane_o = lax.broadcasted_iota(jnp.int32, (2 * blk, LANES), 1) < HEAD_DIM
    zero = jnp.zeros((), BF16)
    ones_lo = jnp.where(lane_lo, 1.0, 0.0).astype(BF16)
    ones_hi = jnp.where(lane_lo, 0.0, 1.0).astype(BF16)
    row3 = lax.broadcasted_iota(jnp.int32, (blk, 3 * blk), 0)
    col3 = lax.broadcasted_iota(jnp.int32, (blk, 3 * blk), 1) - blk
    pool_band = [((col3 >= row3 - w // 2) & (col3 <= row3 + w // 2 - 1)).astype(BF16) for w in POOL_WINDOWS]
    row_pos = lax.broadcasted_iota(jnp.int32, (blk, LANES), 0)

    for j in range(nsub):
        n = tstep * nsub + j
        pblk = jnp.maximum(n - 1, 0)
        nblk = jnp.minimum(n + 1, nb - 1)
        r0 = pl.multiple_of(n * blk, blk)
        ps = pl.multiple_of(pblk * blk, blk)
        ns = pl.multiple_of(nblk * blk, blk)
        has_prev = n > 0
        has_next = n < nb - 1
        rows = slice(j * blk, (j + 1) * blk)

        if band:
            vb = jnp.concatenate([v_ref[pl.ds(ps, blk), :], v_ref[pl.ds(r0, blk), :], v_ref[pl.ds(ns, blk), :],
                                  vc_ref[...]], axis=0)
        else:
            vb = vc_ref[...]
        for hk in range(ATTN_KV_HEADS):
            hd = slice(hk * LANES, (hk + 1) * LANES)
            kt = [kc_ref[i, hd, :] for i in range(kc_ref.shape[0])]
            if band:
                kt = [k_ref[pblk, hd, :], k_ref[n, hd, :], k_ref[nblk, hd, :]] + kt
            kt = jnp.concatenate(kt, axis=1)
            vv = vb[:, hd]
            v_aug = jnp.concatenate(
                [jnp.concatenate([jnp.where(lane_lo, vv, zero), ones_lo], axis=1),
                 jnp.concatenate([jnp.where(lane_lo, zero, vv), ones_hi], axis=1)], axis=0)
            c0, c1 = 2 * hk, 2 * hk + 1
            q2 = jnp.concatenate([q_ref[rows, c0 * LANES:(c0 + 1) * LANES],
                                  q_ref[rows, c1 * LANES:(c1 + 1) * LANES]], axis=0)
            probs, esink = [], []
            for half in range(2):
                s = _dot(q2, jnp.where(dim_lo, kt, zero) if half == 0 else jnp.where(dim_lo, zero, kt))
                if band:
                    s = jnp.concatenate([jnp.where(tri_prev & has_prev, s[:, :blk], NEG), s[:, blk:2 * blk],
                                         jnp.where(tri_next & has_next, s[:, 2 * blk:3 * blk], NEG),
                                         s[:, 3 * blk:]], axis=1)
                snk = jnp.where(top_rows, sink_ref[4 * hk + half], sink_ref[4 * hk + 2 + half])
                m = jnp.maximum(jnp.max(s, axis=-1, keepdims=True), snk)
                probs.append(jnp.exp(s - m).astype(BF16))
                esink.append(jnp.exp(snk - m))
            o2 = _dot(jnp.concatenate(probs, axis=1), v_aug)
            o = o2[:, :LANES] / (o2[:, LANES:] + jnp.where(lane_o, esink[0], esink[1]))
            cat_ref[rows, c0 * LANES:(c0 + 1) * LANES] = o[:blk].astype(BF16)
            cat_ref[rows, c1 * LANES:(c1 + 1) * LANES] = o[blk:].astype(BF16)

        uo = u_ref[pl.ds(r0, blk), :]
        up = jnp.where(has_prev, u_ref[pl.ds(ps, blk), :], zero)
        un = jnp.where(has_next, u_ref[pl.ds(ns, blk), :], zero)
        ub = jnp.concatenate([up, uo, un], axis=0)
        pos = r0 + row_pos
        for gi, w in enumerate(POOL_WINDOWS):
            cs = slice(gi * POOL_CH, (gi + 1) * POOL_CH)
            sums = _dot(pool_band[gi], ub[:, cs])
            cnt = jnp.minimum(pos + w // 2, t_len) - jnp.maximum(pos - w // 2, 0)
            dlt = sums / cnt.astype(F32) - uo[:, cs].astype(F32)
            y = _dot(dlt.astype(BF16), pw_ref[gi]) * ps_ref[:, cs]
            off = ATTN_Q_HEADS * HEAD_DIM + gi * POOL_CH
            cat_ref[rows, off:off + POOL_CH] = y.astype(BF16)

    y = _dot(cat_ref[...], wo_ref[...])
    o_ref[...] = x_ref[...] + mod_ref[2:3, :] * _rms(y, g_ref[...])


def _mix0(x, q, k4, v4, kc4, vc4, u, sink, pool_w, pool_scale, w_out, g, mod, mod_row):
    bsz, t, d = x.shape
    band = k4 is not None
    tq = min(t, 512)
    n_ctx = vc4.shape[1]
    row = (lambda b, i: (b, 0, 0)) if mod_row is None else (lambda b, i: (mod_row, 0, 0))
    full = lambda n, w: pl.BlockSpec((None, n, w), lambda b, i: (b, 0, 0))
    keys_t = lambda a: pl.BlockSpec((None,) + a.shape[1:], lambda b, i: (b, 0, 0, 0))
    in_specs = [
        pl.BlockSpec(memory_space=pltpu.SMEM),
        pl.BlockSpec((None, tq, d), lambda b, i: (b, i, 0)),
        pl.BlockSpec((None, tq, q.shape[2]), lambda b, i: (b, i, 0)),
    ]
    args = [sink, x, q]
    if band:
        in_specs += [keys_t(k4), full(t, v4.shape[2])]
        args += [k4, v4]
    in_specs += [keys_t(kc4), full(n_ctx, vc4.shape[2]), full(t, u.shape[2]),
                 _const_spec(pool_w.shape), _const_spec(pool_scale.shape), _const_spec(w_out.shape),
                 _const_spec(g.shape), pl.BlockSpec((None, 6, d), row)]
    args += [kc4, vc4, u, pool_w, pool_scale, w_out, g, mod]
    return pl.pallas_call(
        functools.partial(_mix0_kernel, band=band, t_len=t, tq=tq),
        out_shape=jax.ShapeDtypeStruct((bsz, t, d), F32),
        grid=(bsz, t // tq),
        in_specs=in_specs,
        out_specs=pl.BlockSpec((None, tq, d), lambda b, i: (b, i, 0)),
        scratch_shapes=[pltpu.VMEM((tq, w_out.shape[0]), BF16)],
        compiler_params=_params("parallel", "parallel"),
        name="mix0_band" if band else "mix0_ctx",
    )(*args)


HALO = SUBLANES
FFN_CHUNK = 256
FFN_ROWS = 512
LIGHT_ROWS = 1024


def _halo_specs(tm, t, d):
    nblk = tm // HALO
    last = t // HALO - 1
    return [
        pl.BlockSpec((None, tm, d), lambda b, i: (b, i, 0)),
        pl.BlockSpec((None, HALO, d), lambda b, i: (b, jnp.maximum(i * nblk - 1, 0), 0)),
        pl.BlockSpec((None, HALO, d), lambda b, i: (b, jnp.minimum((i + 1) * nblk, last), 0)),
    ]


def _conv3(u, cw, cb, tm):
    return (cw[0:1, :] * u[HALO - 1:HALO - 1 + tm, :] + cw[1:2, :] * u[HALO:HALO + tm, :]
            + cw[2:3, :] * u[HALO + 1:HALO + 1 + tm, :] + cb)


def _ffn_kernel(xm_ref, xp_ref, xn_ref, mod_ref, gpre_ref, gpost_ref, wup_ref, cw_ref, cb_ref, wdn_ref,
                o_ref, hb_ref, acc_ref, u0_ref, u1_ref, *, tm, nt):
    i = pl.program_id(1)
    shift, scale, gate = mod_ref[3:4, :], mod_ref[4:5, :], mod_ref[5:6, :]
    gpre = gpre_ref[...]

    def pre(xv):
        return _rms(xv, gpre) * (1.0 + scale) + shift

    hp = jnp.where(i > 0, pre(xp_ref[...]), 0.0)
    hn = jnp.where(i < nt - 1, pre(xn_ref[...]), 0.0)
    hb_ref[...] = jnp.concatenate([hp, pre(xm_ref[...]), hn], axis=0).astype(BF16)
    acc_ref[...] = jnp.zeros_like(acc_ref)
    dff = wdn_ref.shape[0]
    nchunk = dff // FFN_CHUNK
    half = FFN_CHUNK // LANES
    u_refs = (u0_ref, u1_ref)

    def cols(c, j):
        o = (dff if j >= half else 0) + c * FFN_CHUNK + (j % half) * LANES
        return slice(o, o + LANES)

    def up(c, s):
        for part in range(2):
            o = part * dff + c * FFN_CHUNK
            u = _dot(hb_ref[...], wup_ref[:, o:o + FFN_CHUNK])
            for j in range(half):
                u_refs[s][part * half + j] = u[:, j * LANES:(j + 1) * LANES]

    def conv_act_down(c, s):
        ur = u_refs[s]

        def conv(j):
            cs = cols(c, j)
            return (cw_ref[0:1, cs] * ur[j, pl.ds(HALO - 1, tm), :] + cw_ref[1:2, cs] * ur[j, pl.ds(HALO, tm), :]
                    + cw_ref[2:3, cs] * ur[j, pl.ds(HALO + 1, tm), :] + cb_ref[:, cs])

        acts = []
        for j in range(half):
            ag, av = conv(j), conv(half + j)
            acts.append((ag * _sigmoid(ag) * av).astype(BF16))
        acc_ref[...] += _dot(jnp.concatenate(acts, axis=1), wdn_ref[c * FFN_CHUNK:(c + 1) * FFN_CHUNK, :])

    up(0, 0)
    for c in range(nchunk):
        if c + 1 < nchunk:
            up(c + 1, (c + 1) % 2)
        conv_act_down(c, c % 2)
    o_ref[...] = xm_ref[...] + gate * _rms(acc_ref[...], gpost_ref[...])


def _ffn(x, mod, mod_row, gpre, gpost, wup, cw, cb, wdn):
    bsz, t, d = x.shape
    tm = min(t, FFN_ROWS)
    nt = t // tm
    assert wdn.shape[0] % FFN_CHUNK == 0
    row = (lambda b, i: (b, 0, 0)) if mod_row is None else (lambda b, i: (mod_row, 0, 0))
    in_specs = _halo_specs(tm, t, d) + [
        pl.BlockSpec((None, 6, d), row), _const_spec(gpre.shape), _const_spec(gpost.shape),
        _const_spec(wup.shape), _const_spec(cw.shape), _const_spec(cb.shape), _const_spec(wdn.shape)]
    u_buf = pltpu.VMEM((2 * FFN_CHUNK // LANES, tm + 2 * HALO, LANES), F32)
    return pl.pallas_call(
        functools.partial(_ffn_kernel, tm=tm, nt=nt),
        out_shape=jax.ShapeDtypeStruct((bsz, t, d), F32),
        grid=(bsz, nt),
        in_specs=in_specs,
        out_specs=pl.BlockSpec((None, tm, d), lambda b, i: (b, i, 0)),
        scratch_shapes=[pltpu.VMEM((tm + 2 * HALO, d), BF16), pltpu.VMEM((tm, d), F32), u_buf, u_buf],
        compiler_params=_params("parallel", "parallel"),
        name="ffn",
    )(x, x, x, mod, gpre, gpost, wup, cw, cb, wdn)


def _in1_kernel(*refs, tm, nt, full):
    if full:
        (xm_ref, xp_ref, xn_ref, mod_ref, g_ref, w_ref, wg_ref, gb_ref, cw_ref, cb_ref, qw_ref, kw_ref,
         uc_ref, og_ref, v_ref, q_ref, k_ref, gt_ref) = refs
    else:
        (xm_ref, xp_ref, xn_ref, mod_ref, g_ref, w_ref, wg_ref, gb_ref, cw_ref, cb_ref, kw_ref,
         v_ref, k_ref, gt_ref) = refs
    i = pl.program_id(1)
    shift, scale = mod_ref[0:1, :], mod_ref[1:2, :]
    g = g_ref[...]

    def pre(xv):
        return _rms(xv, g) * (1.0 + scale) + shift

    hm = pre(xm_ref[...])
    hp = jnp.where(i > 0, pre(xp_ref[...]), 0.0)
    hn = jnp.where(i < nt - 1, pre(xn_ref[...]), 0.0)
    hext = jnp.concatenate([hp, hm, hn], axis=0).astype(BF16)
    dv = v_ref.shape[1]
    u = _dot(hext, w_ref[:, :dv])
    uc = _conv3(u, cw_ref[...], cb_ref[...], tm)
    uc = uc * _sigmoid(uc)
    ucb = uc.astype(BF16)
    hmb = hm.astype(BF16)
    v_ref[...] = _dot(hmb, w_ref[:, dv:2 * dv]).astype(BF16)
    gt_ref[...] = _dot(hmb, wg_ref[...]) + gb_ref[...]
    if full:
        og_ref[...] = _dot(hmb, w_ref[:, 2 * dv:3 * dv]).astype(BF16)
        uc_ref[...] = ucb
    dh = MLSTM_HEAD_DIM
    for hh in range(MLSTM_HEADS):
        uh = ucb[:, hh * dh:(hh + 1) * dh]
        k_ref[:, hh * dh:(hh + 1) * dh] = _dot(uh, kw_ref[hh]).astype(BF16)
        if full:
            q_ref[:, hh * dh:(hh + 1) * dh] = _dot(uh, qw_ref[hh]).astype(BF16)


def _in1(x, mod, mod_row, g, w, wg, gb, cw, cb, qw, kw, full):
    bsz, t, d = x.shape
    tm = min(t, 512)
    nt = t // tm
    dw = w.shape[1] // 3
    row = (lambda b, i: (b, 0, 0)) if mod_row is None else (lambda b, i: (mod_row, 0, 0))
    in_specs = _halo_specs(tm, t, d) + [pl.BlockSpec((None, 6, d), row)]
    consts = [g, w, wg, gb, cw, cb] + ([qw] if full else []) + [kw]
    in_specs += [_const_spec(a.shape) for a in consts]
    act = lambda: jax.ShapeDtypeStruct((bsz, t, dw), BF16)
    gates = jax.ShapeDtypeStruct((bsz, t, LANES), F32)
    out_shape = [act(), act(), act(), act(), act(), gates] if full else [act(), act(), gates]
    out_specs = [pl.BlockSpec((None, tm, s.shape[2]), lambda b, i: (b, i, 0)) for s in out_shape]
    return pl.pallas_call(
        functools.partial(_in1_kernel, tm=tm, nt=nt, full=full),
        out_shape=out_shape,
        grid=(bsz, nt),
        in_specs=in_specs,
        out_specs=out_specs,
        compiler_params=_params("parallel", "parallel"),
        name="in1_full" if full else "in1_ctx",
    )(x, x, x, mod, *consts)


def _log_sigmoid(x):
    return jnp.minimum(x, 0.0) - jnp.log1p(jnp.exp(-jnp.abs(x)))


def _scan_lanes(x, op, fill, reverse):
    n = x.shape[1]
    lane = lax.broadcasted_iota(jnp.int32, x.shape, 1)
    d = 1
    while d < n:
        if reverse:
            shifted = jnp.where(lane < n - d, pltpu.roll(x, n - d, 1), fill)
        else:
            shifted = jnp.where(lane >= d, pltpu.roll(x, d, 1), fill)
        x = op(x, shifted)
        d *= 2
    return x


def _mlstm_kernel(q_ref, k_ref, v_ref, kc_ref, vc_ref, gl_ref, gc_ref, og_ref, uc_ref, ng_ref, skip_ref,
                  y_ref, cf_ref, cb_ref, *, nc):
    ln = MLSTM_CHUNK
    dh = q_ref.shape[1]
    nr = nc + 1
    gates = [jnp.concatenate([gc_ref[gi], gl_ref[gi]], axis=0) for gi in range(4)]

    def direction(ig, fpre, reverse):
        b = _scan_lanes(_log_sigmoid(fpre), jnp.add, 0.0, reverse)
        a = ig - b
        cmax = _scan_lanes(a, jnp.maximum, NEG, reverse)
        e = 0 if reverse else ln - 1
        b_end, a_max = b[:, e:e + 1], cmax[:, e:e + 1]
        order = [0] + (list(range(nc, 0, -1)) if reverse else list(range(1, nr)))
        m_in = [None] * nr
        m = jnp.zeros((1, 1), F32)
        for r in order:
            m_in[r] = m
            m = b_end[r:r + 1] + jnp.maximum(m, a_max[r:r + 1])
        m_in = jnp.concatenate(m_in, axis=0)
        big_m = jnp.maximum(m_in, cmax)
        m_end = jnp.maximum(m_in, a_max)
        return dict(a=a, M=big_m, mt=b + big_m, m_in=m_in, wts=jnp.exp(a - m_end), decay=jnp.exp(m_in - m_end))

    fw = direction(gates[0], gates[1], False)
    bw = direction(gates[2], gates[3], True)

    def lane_bcast(row):
        return jnp.transpose(jnp.broadcast_to(row, (LANES, ln)))

    def wide(a):
        return jnp.concatenate([a] * (dh // LANES), axis=1)

    ones = jnp.ones((ln, LANES), BF16)

    def chunk_kv(r):
        if r == 0:
            return kc_ref[...], vc_ref[...]
        return k_ref[(r - 1) * ln:r * ln, :], v_ref[(r - 1) * ln:r * ln, :]

    contrib_f, contrib_b = [], []
    for r in range(nr):
        kk, vv = chunk_kv(r)
        kt = jnp.transpose(kk.astype(F32))
        v_aug = jnp.concatenate([vv, ones], axis=1)
        contrib_f.append(_dot((kt * fw["wts"][r:r + 1]).astype(BF16), v_aug))
        contrib_b.append(_dot((kt * bw["wts"][r:r + 1]).astype(BF16), v_aug))

    st = contrib_f[0]
    for c in range(nc):
        cf_ref[c] = st.astype(BF16)
        if c + 1 < nc:
            st = fw["decay"][c + 1:c + 2] * st + contrib_f[c + 1]
    st = contrib_b[0]
    for c in range(nc - 1, -1, -1):
        cb_ref[c] = st.astype(BF16)
        if c > 0:
            st = bw["decay"][c + 1:c + 2] * st + contrib_b[c + 1]

    ti = lax.broadcasted_iota(jnp.int32, (ln, ln), 0)
    sj = lax.broadcasted_iota(jnp.int32, (ln, ln), 1)
    ng = ng_ref[...]
    skip = skip_ref[...]
    for c in range(nc):
        r = c + 1
        qc = q_ref[c * ln:(c + 1) * ln, :]
        kk, vv = chunk_kv(r)
        s = _dot_nt(qc, kk)
        qf = qc.astype(F32)
        lhs = []
        p_sum = None
        for dirn, causal, st_ref in ((fw, sj <= ti, cf_ref), (bw, sj >= ti, cb_ref)):
            big_m = lane_bcast(dirn["M"][r:r + 1])
            p = s * jnp.exp(jnp.where(causal, dirn["a"][r:r + 1] - wide(big_m), NEG))
            iw = jnp.exp(dirn["m_in"][r:r + 1] - big_m)
            den = _dot(p.astype(BF16), ones) + iw * _dot(qc, st_ref[c, :, dh:])
            rinv = 1.0 / jnp.maximum(jnp.abs(den), jnp.exp(-lane_bcast(dirn["mt"][r:r + 1])))
            p_sum = p * wide(rinv) if p_sum is None else p_sum + p * wide(rinv)
            lhs.append((qf * wide(iw * rinv)).astype(BF16))
        lhs = jnp.concatenate([p_sum.astype(BF16)] + lhs, axis=1)
        rhs = jnp.concatenate([vv, cf_ref[c, :, :dh], cb_ref[c, :, :dh]], axis=0)
        hsum = _dot(lhs, rhs)
        rows = slice(c * ln, (c + 1) * ln)
        y = _sigmoid(og_ref[rows, :].astype(F32)) * (_rms(hsum, ng) + skip * uc_ref[rows, :].astype(F32))
        y_ref[rows, :] = y.astype(BF16)


def _mlstm(q, k, v, kc, vc, gl, gc, og, uc, ng, skip):
    bsz, t, wd = q.shape
    dh = MLSTM_HEAD_DIM
    nh = wd // dh
    nc = t // MLSTM_CHUNK
    n_ctx = kc.shape[1]
    assert n_ctx == MLSTM_CHUNK and gl.shape == (bsz, nh, 4, nc, MLSTM_CHUNK)
    seq = lambda n: pl.BlockSpec((None, n, dh), lambda b, h: (b, 0, h))
    head_vec = pl.BlockSpec((1, dh), lambda b, h: (0, h))
    state = pltpu.VMEM((nc, dh, dh + LANES), BF16)
    return pl.pallas_call(
        functools.partial(_mlstm_kernel, nc=nc),
        out_shape=jax.ShapeDtypeStruct((bsz, t, wd), BF16),
        grid=(bsz, nh),
        in_specs=[seq(t), seq(t), seq(t), seq(n_ctx), seq(n_ctx),
                  pl.BlockSpec((None, None, 4, nc, MLSTM_CHUNK), lambda b, h: (b, h, 0, 0, 0)),
                  pl.BlockSpec((None, None, 4, 1, MLSTM_CHUNK), lambda b, h: (b, h, 0, 0, 0)),
                  seq(t), seq(t), head_vec, head_vec],
        out_specs=seq(t),
        scratch_shapes=[state, state],
        compiler_params=_params("parallel", "parallel"),
        name="mlstm",
    )(q, k, v, kc, vc, gl, gc, og, uc, ng, skip)


def _out1_kernel(x_ref, y_ref, wo_ref, g_ref, mod_ref, o_ref):
    y = _dot(y_ref[...], wo_ref[...])
    o_ref[...] = x_ref[...] + mod_ref[2:3, :] * _rms(y, g_ref[...])


def _out1(x, y, w_out, g, mod):
    bsz, t, d = x.shape
    tm = min(t, LIGHT_ROWS)
    wd = y.shape[2]
    tile = lambda n: pl.BlockSpec((None, tm, n), lambda b, i: (b, i, 0))
    return pl.pallas_call(
        _out1_kernel,
        out_shape=jax.ShapeDtypeStruct((bsz, t, d), F32),
        grid=(bsz, t // tm),
        in_specs=[tile(d), tile(wd), _const_spec(w_out.shape), _const_spec(g.shape),
                  pl.BlockSpec((None, 6, d), lambda b, i: (b, 0, 0))],
        out_specs=tile(d),
        compiler_params=_params("parallel", "parallel"),
        name="out1",
    )(x, y, w_out, g, mod)


def _rope_tables(t_len, scale):
    rows = t_len // GRID_W
    row = jnp.repeat(jnp.arange(rows, dtype=F32), GRID_W)
    col = jnp.tile(jnp.arange(GRID_W, dtype=F32), rows)
    n_freq = HEAD_DIM // 4
    inv = ROPE_BASE ** (-jnp.arange(n_freq, dtype=F32) / n_freq)
    ar, ac = row[:, None] * inv, col[:, None] * inv
    cos = jnp.concatenate([jnp.cos(ar), jnp.cos(ar), jnp.cos(ac), jnp.cos(ac)], axis=-1)
    sin = jnp.concatenate([-jnp.sin(ar), jnp.sin(ar), -jnp.sin(ac), jnp.sin(ac)], axis=-1)
    reps = LANES // HEAD_DIM
    return jnp.tile(cos, (1, reps)) * scale, jnp.tile(sin, (1, reps)) * scale


def _ffn_weights(w_up, conv_w, conv_b, w_down):
    return w_up.astype(BF16), conv_w, conv_b.reshape(1, -1), w_down.astype(BF16)


def kernel(x, c, ctx, c_ctx, mod_w, mod_b, norm_g, attn_in_w, attn_sink, pool_w, pool_scale, attn_out_w,
           rec_in_w, rec_gate_b, rec_conv_w, rec_conv_b, rec_q_w, rec_k_w, rec_norm_g, rec_skip, rec_out_w,
           ffn_up_w, ffn_conv_w, ffn_conv_b, ffn_down_w):
    bsz, t, d = x.shape
    n_ctx = ctx.shape[1]
    ctx_row = bsz

    pad_rows = (-(bsz + 1)) % 16
    c_all = jnp.concatenate([c, c_ctx[None, :], jnp.zeros((pad_rows, d), F32)], axis=0)
    mod = _adaln(c_all, mod_w, mod_b)
    mod = mod.reshape(mod.shape[0], mod.shape[1], 6, d)
    row2 = lambda a: a.reshape(1, -1)

    w_in0 = attn_in_w[0].astype(BF16)
    qscale = HEAD_DIM ** -0.5
    tables = _rope_tables(t, qscale) + _rope_tables(t, 1.0)
    g00 = row2(norm_g[0, 0])
    q, k4, v4, u = _in0(x, mod[0], None, g00, w_in0, tables)
    qc, kc4, vc4, ucx = _in0(ctx, mod[0], ctx_row, g00, w_in0, None)
    pw = pool_w[0].astype(BF16)
    psc = row2(pool_scale[0])
    wo0 = attn_out_w[0].astype(BF16)
    g01 = row2(norm_g[0, 1])
    sink = attn_sink[0]
    x1 = _mix0(x, q, k4, v4, kc4, vc4, u, sink, pw, psc, wo0, g01, mod[0], None)
    c1 = _mix0(ctx, qc, None, None, kc4, vc4, ucx, sink, pw, psc, wo0, g01, mod[0], ctx_row)
    f0 = _ffn_weights(ffn_up_w[0], ffn_conv_w[0], ffn_conv_b[0], ffn_down_w[0])
    g02, g03 = row2(norm_g[0, 2]), row2(norm_g[0, 3])
    x2 = _ffn(x1, mod[0], None, g02, g03, *f0)
    c2 = _ffn(c1, mod[0], ctx_row, g02, g03, *f0)

    w = rec_in_w[0]
    wd = MLSTM_HEADS * MLSTM_HEAD_DIM
    ng = 4 * MLSTM_HEADS
    w_main = w[:, :3 * wd].astype(BF16)
    w_gate = jnp.concatenate([w[:, 3 * wd:], jnp.zeros((d, LANES - ng), F32)], axis=1).astype(BF16)
    gbias = jnp.concatenate([rec_gate_b[0].reshape(1, ng), jnp.zeros((1, LANES - ng), F32)], axis=1)
    cw, cb = rec_conv_w[0], row2(rec_conv_b[0])
    qw = rec_q_w[0].astype(BF16)
    kw = (rec_k_w[0] * (MLSTM_HEAD_DIM ** -0.5)).astype(BF16)
    g10 = row2(norm_g[1, 0])
    uc, og, v, qm, km, gts = _in1(x2, mod[1], None, g10, w_main, w_gate, gbias, cw, cb, qw, kw, True)
    vcx, kcx, gtc = _in1(c2, mod[1], ctx_row, g10, w_main, w_gate, gbias, cw, cb, None, kw, False)

    def gate_rows(gt, n):
        gt = gt[:, :, :ng].reshape(bsz, n // MLSTM_CHUNK, MLSTM_CHUNK, 4, MLSTM_HEADS)
        return gt.transpose(0, 4, 3, 1, 2)

    y = _mlstm(qm, km, v, kcx, vcx, gate_rows(gts, t), gate_rows(gtc, n_ctx), og, uc,
               row2(rec_norm_g[0]), row2(rec_skip[0]))
    x3 = _out1(x2, y, rec_out_w[0].astype(BF16), row2(norm_g[1, 1]), mod[1])
    f1 = _ffn_weights(ffn_up_w[1], ffn_conv_w[1], ffn_conv_b[1], ffn_down_w[1])
    return _ffn(x3, mod[1], None, row2(norm_g[1, 2]), row2(norm_g[1, 3]), *f1)
```

```python
import functools

import jax
import jax.numpy as jnp
from jax import lax
from jax.experimental import pallas as pl
from jax.experimental.pallas import tpu as pltpu

F32 = jnp.float32
BF16 = jnp.bfloat16

GRID_W = 64
HEAD_DIM = 64
ATTN_Q_HEADS = 8
ATTN_KV_HEADS = 2
ATTN_BLOCK = 128
ROPE_BASE = 10000.0
POOL_WINDOWS = (2, 4, 8, 16)
POOL_CH = 128
MLSTM_HEADS = 4
MLSTM_HEAD_DIM = 256
MLSTM_CHUNK = 256
MLSTM_HEADS_PER_STEP = 2
EPS = 1e-6
NEG = -1e30

LANES = 128
SUBLANES = 8
VMEM_LIMIT = 56 * 1024 * 1024


def _params(*sem):
    return pltpu.CompilerParams(dimension_semantics=sem, vmem_limit_bytes=VMEM_LIMIT)


def _rms(x, g):
    return x * lax.rsqrt(jnp.mean(x * x, axis=-1, keepdims=True) + EPS) * g


def _sigmoid(x):
    return 1.0 / (1.0 + jnp.exp(-x))


def _dot(a, b):
    return jnp.dot(a, b, preferred_element_type=F32)


def _dot_nt(a, b):
    return lax.dot_general(a, b, (((1,), (1,)), ((), ())), preferred_element_type=F32)


def _dot_tn(a, b):
    return lax.dot_general(a, b, (((0,), (0,)), ((), ())), preferred_element_type=F32)


def _const_spec(shape):
    nd = len(shape)
    return pl.BlockSpec(shape, lambda *_: (0,) * nd)


def _adaln_kernel(c_ref, w_ref, b_ref, o_ref):
    c = c_ref[...]
    s = c * _sigmoid(c)
    o_ref[...] = _dot(s.astype(BF16), w_ref[...].astype(BF16)) + b_ref[...]


def _adaln(c_all, mod_w, mod_b):
    depth, d, n6 = mod_w.shape
    rows = c_all.shape[0]
    tn = 1536
    return pl.pallas_call(
        _adaln_kernel,
        out_shape=jax.ShapeDtypeStruct((depth, rows, n6), F32),
        grid=(depth, n6 // tn),
        in_specs=[
            pl.BlockSpec((rows, d), lambda l, j: (0, 0)),
            pl.BlockSpec((None, d, tn), lambda l, j: (l, 0, j)),
            pl.BlockSpec((None, 1, tn), lambda l, j: (l, 0, j)),
        ],
        out_specs=pl.BlockSpec((None, rows, tn), lambda l, j: (l, 0, j)),
        compiler_params=_params("arbitrary", "arbitrary"),
        name="adaln",
    )(c_all, mod_w, mod_b.reshape(depth, 1, n6))


def _in0_kernel(*refs, rope):
    if rope:
        x_ref, mod_ref, g_ref, w_ref, cq_ref, sq_ref, ck_ref, sk_ref, q_ref, k_ref, v_ref, u_ref = refs
    else:
        x_ref, mod_ref, g_ref, w_ref, q_ref, k_ref, v_ref, u_ref = refs
    x = x_ref[...]
    h = _rms(x, g_ref[...]) * (1.0 + mod_ref[1:2, :]) + mod_ref[0:1, :]
    y = _dot(h.astype(BF16), w_ref[...])
    nq = q_ref.shape[1]
    nkv = ATTN_KV_HEADS * HEAD_DIM
    tm = x.shape[0]
    lane = lax.broadcasted_iota(jnp.int32, (tm, LANES), 1)
    first = lane < HEAD_DIM

    def dup(a):
        sw = pltpu.roll(a, HEAD_DIM, 1)
        return jnp.concatenate([jnp.where(first, a, sw), jnp.where(first, sw, a)], axis=1)

    def put_keys(kf):
        for jb in range(tm // ATTN_BLOCK):
            k_ref[jb] = jnp.transpose(kf[jb * ATTN_BLOCK:(jb + 1) * ATTN_BLOCK, :]).astype(BF16)

    if rope:
        even = (lane // 16) % 2 == 0

        def rot(a, c, s):
            sw = jnp.where(even, pltpu.roll(a, LANES - 16, 1), pltpu.roll(a, 16, 1))
            return a * c + sw * s

        cq, sq, ck, sk = cq_ref[...], sq_ref[...], ck_ref[...], sk_ref[...]
        for j in range(nq // LANES):
            q_ref[:, j * LANES:(j + 1) * LANES] = rot(y[:, j * LANES:(j + 1) * LANES], cq, sq).astype(BF16)
        put_keys(dup(rot(y[:, nq:nq + nkv], ck, sk)))
    else:
        q_ref[...] = (y[:, :nq] * (HEAD_DIM ** -0.5)).astype(BF16)
        put_keys(dup(y[:, nq:nq + nkv]))
    v_ref[...] = dup(y[:, nq + nkv:nq + 2 * nkv]).astype(BF16)
    u_ref[...] = y[:, nq + 2 * nkv:].astype(BF16)


def _in0(x, mod, mod_row, g, w, tables):
    bsz, t, d = x.shape
    tm = min(t, LIGHT_ROWS)
    rope = tables is not None
    nq, nk, nu = 512, 256, 512
    row = (lambda b, i: (b, 0, 0)) if mod_row is None else (lambda b, i: (mod_row, 0, 0))
    in_specs = [
        pl.BlockSpec((None, tm, d), lambda b, i: (b, i, 0)),
        pl.BlockSpec((None, 6, d), row),
        _const_spec((1, d)),
        _const_spec(w.shape),
    ]
    args = [x, mod, g, w]
    if rope:
        in_specs += [pl.BlockSpec((tm, LANES), lambda b, i: (i, 0))] * 4
        args += list(tables)
    rows = lambda n: (jax.ShapeDtypeStruct((bsz, t, n), BF16), pl.BlockSpec((None, tm, n), lambda b, i: (b, i, 0)))
    keys_t = (jax.ShapeDtypeStruct((bsz, t // ATTN_BLOCK, nk, ATTN_BLOCK), BF16),
              pl.BlockSpec((None, tm // ATTN_BLOCK, nk, ATTN_BLOCK), lambda b, i: (b, i, 0, 0)))
    outs = [rows(nq), keys_t, rows(nk), rows(nu)]
    return pl.pallas_call(
        functools.partial(_in0_kernel, rope=rope),
        out_shape=[o[0] for o in outs],
        grid=(bsz, t // tm),
        in_specs=in_specs,
        out_specs=[o[1] for o in outs],
        compiler_params=_params("parallel", "parallel"),
        name="in0_rope" if rope else "in0_ctx",
    )(*args)


def _mix0_kernel(*refs, band, t_len, tq):
    if band:
        (sink_ref, x_ref, q_ref, k_ref, v_ref, kc_ref, vc_ref, u_ref, pw_ref, ps_ref, wo_ref, g_ref, mod_ref,
         o_ref, cat_ref) = refs
    else:
        (sink_ref, x_ref, q_ref, kc_ref, vc_ref, u_ref, pw_ref, ps_ref, wo_ref, g_ref, mod_ref,
         o_ref, cat_ref) = refs
    blk = ATTN_BLOCK
    nb = t_len // blk
    nsub = tq // blk
    tstep = pl.program_id(1)
    n_ctx = vc_ref.shape[0]
    n_keys = (3 * blk if band else 0) + n_ctx

    qrow = jnp.bitwise_and(lax.broadcasted_iota(jnp.int32, (2 * blk, blk), 0), blk - 1)
    kcol = lax.broadcasted_iota(jnp.int32, (2 * blk, blk), 1)
    tri_prev = kcol >= qrow
    tri_next = kcol <= qrow
    top_rows = lax.broadcasted_iota(jnp.int32, (2 * blk, 1), 0) < blk
    dim_lo = lax.broadcasted_iota(jnp.int32, (LANES, n_keys), 0) < HEAD_DIM
    lane_lo = lax.broadcasted_iota(jnp.int32, (n_keys, LANES), 1) < HEAD_DIM
    lane_o = lax.broadcasted_iota(jnp.int32, (2 * blk, LANES), 1) < HEAD_DIM
    zero = jnp.zeros((), BF16)
    ones_lo = jnp.where(lane_lo, 1.0, 0.0).astype(BF16)
    ones_hi = jnp.where(lane_lo, 0.0, 1.0).astype(BF16)
    row3 = lax.broadcasted_iota(jnp.int32, (blk, 3 * blk), 0)
    col3 = lax.broadcasted_iota(jnp.int32, (blk, 3 * blk), 1) - blk
    pool_band = [((col3 >= row3 - w // 2) & (col3 <= row3 + w // 2 - 1)).astype(BF16) for w in POOL_WINDOWS]
    row_pos = lax.broadcasted_iota(jnp.int32, (blk, LANES), 0)

    for j in range(nsub):
        n = tstep * nsub + j
        pblk = jnp.maximum(n - 1, 0)
        nblk = jnp.minimum(n + 1, nb - 1)
        r0 = pl.multiple_of(n * blk, blk)
        ps = pl.multiple_of(pblk * blk, blk)
        ns = pl.multiple_of(nblk * blk, blk)
        has_prev = n > 0
        has_next = n < nb - 1
        rows = slice(j * blk, (j + 1) * blk)

        if band:
            vb = jnp.concatenate([v_ref[pl.ds(ps, blk), :], v_ref[pl.ds(r0, blk), :], v_ref[pl.ds(ns, blk), :],
                                  vc_ref[...]], axis=0)
        else:
            vb = vc_ref[...]
        for hk in range(ATTN_KV_HEADS):
            hd = slice(hk * LANES, (hk + 1) * LANES)
            kt = [kc_ref[i, hd, :] for i in range(kc_ref.shape[0])]
            if band:
                kt = [k_ref[pblk, hd, :], k_ref[n, hd, :], k_ref[nblk, hd, :]] + kt
            kt = jnp.concatenate(kt, axis=1)
            vv = vb[:, hd]
            v_aug = jnp.concatenate(
                [jnp.concatenate([jnp.where(lane_lo, vv, zero), ones_lo], axis=1),
                 jnp.concatenate([jnp.where(lane_lo, zero, vv), ones_hi], axis=1)], axis=0)
            c0, c1 = 2 * hk, 2 * hk + 1
            q2 = jnp.concatenate([q_ref[rows, c0 * LANES:(c0 + 1) * LANES],
                                  q_ref[rows, c1 * LANES:(c1 + 1) * LANES]], axis=0)
            probs, esink = [], []
            for half in range(2):
                s = _dot(q2, jnp.where(dim_lo, kt, zero) if half == 0 else jnp.where(dim_lo, zero, kt))
                if band:
                    s = jnp.concatenate([jnp.where(tri_prev & has_prev, s[:, :blk], NEG), s[:, blk:2 * blk],
                                         jnp.where(tri_next & has_next, s[:, 2 * blk:3 * blk], NEG),
                                         s[:, 3 * blk:]], axis=1)
                snk = jnp.where(top_rows, sink_ref[4 * hk + half], sink_ref[4 * hk + 2 + half])
                m = jnp.maximum(jnp.max(s, axis=-1, keepdims=True), snk)
                probs.append(jnp.exp(s - m).astype(BF16))
                esink.append(jnp.exp(snk - m))
            o2 = _dot(jnp.concatenate(probs, axis=1), v_aug)
            o = o2[:, :LANES] / (o2[:, LANES:] + jnp.where(lane_o, esink[0], esink[1]))
            cat_ref[rows, c0 * LANES:(c0 + 1) * LANES] = o[:blk].astype(BF16)
            cat_ref[rows, c1 * LANES:(c1 + 1) * LANES] = o[blk:].astype(BF16)

        uo = u_ref[pl.ds(r0, blk), :]
        up = jnp.where(has_prev, u_ref[pl.ds(ps, blk), :], zero)
        un = jnp.where(has_next, u_ref[pl.ds(ns, blk), :], zero)
        ub = jnp.concatenate([up, uo, un], axis=0)
        pos = r0 + row_pos
        for gi, w in enumerate(POOL_WINDOWS):
            cs = slice(gi * POOL_CH, (gi + 1) * POOL_CH)
            sums = _dot(pool_band[gi], ub[:, cs])
            cnt = jnp.minimum(pos + w // 2, t_len) - jnp.maximum(pos - w // 2, 0)
            dlt = sums / cnt.astype(F32) - uo[:, cs].astype(F32)
            y = _dot(dlt.astype(BF16), pw_ref[gi]) * ps_ref[:, cs]
            off = ATTN_Q_HEADS * HEAD_DIM + gi * POOL_CH
            cat_ref[rows, off:off + POOL_CH] = y.astype(BF16)

    y = _dot(cat_ref[...], wo_ref[...])
    o_ref[...] = x_ref[...] + mod_ref[2:3, :] * _rms(y, g_ref[...])


def _mix0(x, q, k4, v4, kc4, vc4, u, sink, pool_w, pool_scale, w_out, g, mod, mod_row):
    bsz, t, d = x.shape
    band = k4 is not None
    tq = min(t, LIGHT_ROWS)
    n_ctx = vc4.shape[1]
    row = (lambda b, i: (b, 0, 0)) if mod_row is None else (lambda b, i: (mod_row, 0, 0))
    full = lambda n, w: pl.BlockSpec((None, n, w), lambda b, i: (b, 0, 0))
    keys_t = lambda a: pl.BlockSpec((None,) + a.shape[1:], lambda b, i: (b, 0, 0, 0))
    in_specs = [
        pl.BlockSpec(memory_space=pltpu.SMEM),
        pl.BlockSpec((None, tq, d), lambda b, i: (b, i, 0)),
        pl.BlockSpec((None, tq, q.shape[2]), lambda b, i: (b, i, 0)),
    ]
    args = [sink, x, q]
    if band:
        in_specs += [keys_t(k4), full(t, v4.shape[2])]
        args += [k4, v4]
    in_specs += [keys_t(kc4), full(n_ctx, vc4.shape[2]), full(t, u.shape[2]),
                 _const_spec(pool_w.shape), _const_spec(pool_scale.shape), _const_spec(w_out.shape),
                 _const_spec(g.shape), pl.BlockSpec((None, 6, d), row)]
    args += [kc4, vc4, u, pool_w, pool_scale, w_out, g, mod]
    return pl.pallas_call(
        functools.partial(_mix0_kernel, band=band, t_len=t, tq=tq),
        out_shape=jax.ShapeDtypeStruct((bsz, t, d), F32),
        grid=(bsz, t // tq),
        in_specs=in_specs,
        out_specs=pl.BlockSpec((None, tq, d), lambda b, i: (b, i, 0)),
        scratch_shapes=[pltpu.VMEM((tq, w_out.shape[0]), BF16)],
        compiler_params=_params("parallel", "parallel"),
        name="mix0_band" if band else "mix0_ctx",
    )(*args)


HALO = SUBLANES
FFN_CHUNK = 256
FFN_ROWS = 512
LIGHT_ROWS = 1024


def _halo_specs(tm, t, d):
    nblk = tm // HALO
    last = t // HALO - 1
    return [
        pl.BlockSpec((None, tm, d), lambda b, i: (b, i, 0)),
        pl.BlockSpec((None, HALO, d), lambda b, i: (b, jnp.maximum(i * nblk - 1, 0), 0)),
        pl.BlockSpec((None, HALO, d), lambda b, i: (b, jnp.minimum((i + 1) * nblk, last), 0)),
    ]


def _conv3(u, cw, cb, tm):
    return (cw[0:1, :] * u[HALO - 1:HALO - 1 + tm, :] + cw[1:2, :] * u[HALO:HALO + tm, :]
            + cw[2:3, :] * u[HALO + 1:HALO + 1 + tm, :] + cb)


def _ffn_kernel(xm_ref, xp_ref, xn_ref, mod_ref, gpre_ref, gpost_ref, wup_ref, cw_ref, cb_ref, wdn_ref,
                o_ref, hb_ref, acc_ref, u0_ref, u1_ref, *, tm, nt):
    i = pl.program_id(1)
    shift, scale, gate = mod_ref[3:4, :], mod_ref[4:5, :], mod_ref[5:6, :]
    gpre = gpre_ref[...]

    def pre(xv):
        return _rms(xv, gpre) * (1.0 + scale) + shift

    hp = jnp.where(i > 0, pre(xp_ref[...]), 0.0)
    hn = jnp.where(i < nt - 1, pre(xn_ref[...]), 0.0)
    hb_ref[...] = jnp.concatenate([hp, pre(xm_ref[...]), hn], axis=0).astype(BF16)
    acc_ref[...] = jnp.zeros_like(acc_ref)
    dff = wdn_ref.shape[0]
    nchunk = dff // FFN_CHUNK
    half = FFN_CHUNK // LANES
    u_refs = (u0_ref, u1_ref)

    def cols(c, j):
        o = (dff if j >= half else 0) + c * FFN_CHUNK + (j % half) * LANES
        return slice(o, o + LANES)

    def up(c, s):
        for part in range(2):
            o = part * dff + c * FFN_CHUNK
            u = _dot(hb_ref[...], wup_ref[:, o:o + FFN_CHUNK])
            for j in range(half):
                u_refs[s][part * half + j] = u[:, j * LANES:(j + 1) * LANES]

    def conv_act_down(c, s):
        ur = u_refs[s]

        def conv(j):
            cs = cols(c, j)
            return (cw_ref[0:1, cs] * ur[j, pl.ds(HALO - 1, tm), :] + cw_ref[1:2, cs] * ur[j, pl.ds(HALO, tm), :]
                    + cw_ref[2:3, cs] * ur[j, pl.ds(HALO + 1, tm), :] + cb_ref[:, cs])

        acts = []
        for j in range(half):
            ag, av = conv(j), conv(half + j)
            acts.append((ag * _sigmoid(ag) * av).astype(BF16))
        acc_ref[...] += _dot(jnp.concatenate(acts, axis=1), wdn_ref[c * FFN_CHUNK:(c + 1) * FFN_CHUNK, :])

    up(0, 0)
    for c in range(nchunk):
        if c + 1 < nchunk:
            up(c + 1, (c + 1) % 2)
        conv_act_down(c, c % 2)
    o_ref[...] = xm_ref[...] + gate * _rms(acc_ref[...], gpost_ref[...])


def _ffn(x, mod, mod_row, gpre, gpost, wup, cw, cb, wdn):
    bsz, t, d = x.shape
    tm = min(t, FFN_ROWS)
    nt = t // tm
    assert wdn.shape[0] % FFN_CHUNK == 0
    row = (lambda b, i: (b, 0, 0)) if mod_row is None else (lambda b, i: (mod_row, 0, 0))
    in_specs = _halo_specs(tm, t, d) + [
        pl.BlockSpec((None, 6, d), row), _const_spec(gpre.shape), _const_spec(gpost.shape),
        _const_spec(wup.shape), _const_spec(cw.shape), _const_spec(cb.shape), _const_spec(wdn.shape)]
    u_buf = pltpu.VMEM((2 * FFN_CHUNK // LANES, tm + 2 * HALO, LANES), F32)
    return pl.pallas_call(
        functools.partial(_ffn_kernel, tm=tm, nt=nt),
        out_shape=jax.ShapeDtypeStruct((bsz, t, d), F32),
        grid=(bsz, nt),
        in_specs=in_specs,
        out_specs=pl.BlockSpec((None, tm, d), lambda b, i: (b, i, 0)),
        scratch_shapes=[pltpu.VMEM((tm + 2 * HALO, d), BF16), pltpu.VMEM((tm, d), F32), u_buf, u_buf],
        compiler_params=_params("parallel", "parallel"),
        name="ffn",
    )(x, x, x, mod, gpre, gpost, wup, cw, cb, wdn)


def _in1_kernel(*refs, tm, nt, full):
    if full:
        (xm_ref, xp_ref, xn_ref, mod_ref, g_ref, w_ref, wg_ref, gb_ref, cw_ref, cb_ref, qw_ref, kw_ref,
         uc_ref, og_ref, v_ref, q_ref, k_ref, gt_ref) = refs
    else:
        (xm_ref, xp_ref, xn_ref, mod_ref, g_ref, w_ref, wg_ref, gb_ref, cw_ref, cb_ref, kw_ref,
         v_ref, k_ref, gt_ref) = refs
    i = pl.program_id(1)
    shift, scale = mod_ref[0:1, :], mod_ref[1:2, :]
    g = g_ref[...]

    def pre(xv):
        return _rms(xv, g) * (1.0 + scale) + shift

    hm = pre(xm_ref[...])
    hp = jnp.where(i > 0, pre(xp_ref[...]), 0.0)
    hn = jnp.where(i < nt - 1, pre(xn_ref[...]), 0.0)
    hext = jnp.concatenate([hp, hm, hn], axis=0).astype(BF16)
    dv = v_ref.shape[1]
    u = _dot(hext, w_ref[:, :dv])
    uc = _conv3(u, cw_ref[...], cb_ref[...], tm)
    uc = uc * _sigmoid(uc)
    ucb = uc.astype(BF16)
    hmb = hm.astype(BF16)
    v_ref[...] = _dot(hmb, w_ref[:, dv:2 * dv]).astype(BF16)
    gt_ref[...] = _dot(hmb, wg_ref[...]) + gb_ref[...]
    if full:
        og_ref[...] = _dot(hmb, w_ref[:, 2 * dv:3 * dv]).astype(BF16)
        uc_ref[...] = ucb
    dh = MLSTM_HEAD_DIM
    for hh in range(MLSTM_HEADS):
        uh = ucb[:, hh * dh:(hh + 1) * dh]
        k_ref[:, hh * dh:(hh + 1) * dh] = _dot(uh, kw_ref[hh]).astype(BF16)
        if full:
            q_ref[:, hh * dh:(hh + 1) * dh] = _dot(uh, qw_ref[hh]).astype(BF16)


def _in1(x, mod, mod_row, g, w, wg, gb, cw, cb, qw, kw, full):
    bsz, t, d = x.shape
    tm = min(t, 512)
    nt = t // tm
    dw = w.shape[1] // 3
    row = (lambda b, i: (b, 0, 0)) if mod_row is None else (lambda b, i: (mod_row, 0, 0))
    in_specs = _halo_specs(tm, t, d) + [pl.BlockSpec((None, 6, d), row)]
    consts = [g, w, wg, gb, cw, cb] + ([qw] if full else []) + [kw]
    in_specs += [_const_spec(a.shape) for a in consts]
    act = lambda: jax.ShapeDtypeStruct((bsz, t, dw), BF16)
    gates = jax.ShapeDtypeStruct((bsz, t, LANES), F32)
    out_shape = [act(), act(), act(), act(), act(), gates] if full else [act(), act(), gates]
    out_specs = [pl.BlockSpec((None, tm, s.shape[2]), lambda b, i: (b, i, 0)) for s in out_shape]
    return pl.pallas_call(
        functools.partial(_in1_kernel, tm=tm, nt=nt, full=full),
        out_shape=out_shape,
        grid=(bsz, nt),
        in_specs=in_specs,
        out_specs=out_specs,
        compiler_params=_params("parallel", "parallel"),
        name="in1_full" if full else "in1_ctx",
    )(x, x, x, mod, *consts)


def _log_sigmoid(x):
    return jnp.minimum(x, 0.0) - jnp.log1p(jnp.exp(-jnp.abs(x)))


def _scan_lanes(x, op, fill, reverse):
    n = x.shape[1]
    lane = lax.broadcasted_iota(jnp.int32, x.shape, 1)
    d = 1
    while d < n:
        if reverse:
            shifted = jnp.where(lane < n - d, pltpu.roll(x, n - d, 1), fill)
        else:
            shifted = jnp.where(lane >= d, pltpu.roll(x, d, 1), fill)
        x = op(x, shifted)
        d *= 2
    return x


def _mlstm_kernel(*refs, nc, heads):
    io, states = refs[:12], refs[12:]
    for hh in range(heads):
        _mlstm_head(hh, *io, states[2 * hh], states[2 * hh + 1], nc=nc)


def _mlstm_head(hh, q_ref, k_ref, v_ref, kc_ref, vc_ref, gl_ref, gc_ref, og_ref, uc_ref, ng_ref, skip_ref,
                y_ref, cf_ref, cb_ref, *, nc):
    ln = MLSTM_CHUNK
    dh = MLSTM_HEAD_DIM
    hs = slice(hh * dh, (hh + 1) * dh)
    nr = nc + 1
    scores = [_dot_nt(q_ref[c * ln:(c + 1) * ln, hs], k_ref[c * ln:(c + 1) * ln, hs]) for c in range(nc)]
    gates = [jnp.concatenate([gc_ref[hh, gi], gl_ref[hh, gi]], axis=0) for gi in range(4)]

    def direction(ig, fpre, reverse):
        b = _scan_lanes(_log_sigmoid(fpre), jnp.add, 0.0, reverse)
        a = ig - b
        cmax = _scan_lanes(a, jnp.maximum, NEG, reverse)
        e = 0 if reverse else ln - 1
        b_end, a_max = b[:, e:e + 1], cmax[:, e:e + 1]
        order = [0] + (list(range(nc, 0, -1)) if reverse else list(range(1, nr)))
        m_in = [None] * nr
        m = jnp.zeros((1, 1), F32)
        for r in order:
            m_in[r] = m
            m = b_end[r:r + 1] + jnp.maximum(m, a_max[r:r + 1])
        m_in = jnp.concatenate(m_in, axis=0)
        big_m = jnp.maximum(m_in, cmax)
        m_end = jnp.maximum(m_in, a_max)
        return dict(a=a, M=big_m, mt=b + big_m, m_in=m_in, wts=jnp.exp(a - m_end), decay=jnp.exp(m_in - m_end))

    fw = direction(gates[0], gates[1], False)
    bw = direction(gates[2], gates[3], True)

    def lane_bcast(row):
        return jnp.transpose(jnp.broadcast_to(row, (LANES, ln)))

    def wide(a):
        return jnp.concatenate([a] * (dh // LANES), axis=1)

    ones = jnp.ones((ln, LANES), BF16)

    def chunk_kv(r):
        if r == 0:
            return kc_ref[:, hs], vc_ref[:, hs]
        return k_ref[(r - 1) * ln:r * ln, hs], v_ref[(r - 1) * ln:r * ln, hs]

    contrib_f, contrib_b = [], []
    for r in range(nr):
        kk, vv = chunk_kv(r)
        kt = jnp.transpose(kk.astype(F32))
        v_aug = jnp.concatenate([vv, ones], axis=1)
        contrib_f.append(_dot((kt * fw["wts"][r:r + 1]).astype(BF16), v_aug))
        contrib_b.append(_dot((kt * bw["wts"][r:r + 1]).astype(BF16), v_aug))

    st = contrib_f[0]
    for c in range(nc):
        cf_ref[c] = st.astype(BF16)
        if c + 1 < nc:
            st = fw["decay"][c + 1:c + 2] * st + contrib_f[c + 1]
    st = contrib_b[0]
    for c in range(nc - 1, -1, -1):
        cb_ref[c] = st.astype(BF16)
        if c > 0:
            st = bw["decay"][c + 1:c + 2] * st + contrib_b[c + 1]

    ti = lax.broadcasted_iota(jnp.int32, (ln, ln), 0)
    sj = lax.broadcasted_iota(jnp.int32, (ln, ln), 1)
    ng = ng_ref[:, hs]
    skip = skip_ref[:, hs]
    for c in range(nc):
        r = c + 1
        qc = q_ref[c * ln:(c + 1) * ln, hs]
        kk, vv = chunk_kv(r)
        s = scores[c]
        qf = qc.astype(F32)
        lhs = []
        p_sum = None
        for dirn, causal, st_ref in ((fw, sj <= ti, cf_ref), (bw, sj >= ti, cb_ref)):
            big_m = lane_bcast(dirn["M"][r:r + 1])
            p = s * jnp.exp(jnp.where(causal, dirn["a"][r:r + 1] - wide(big_m), NEG))
            iw = jnp.exp(dirn["m_in"][r:r + 1] - big_m)
            den = _dot(p.astype(BF16), ones) + iw * _dot(qc, st_ref[c, :, dh:])
            rinv = 1.0 / jnp.maximum(jnp.abs(den), jnp.exp(-lane_bcast(dirn["mt"][r:r + 1])))
            p_sum = p * wide(rinv) if p_sum is None else p_sum + p * wide(rinv)
            lhs.append((qf * wide(iw * rinv)).astype(BF16))
        lhs = jnp.concatenate([p_sum.astype(BF16)] + lhs, axis=1)
        rhs = jnp.concatenate([vv, cf_ref[c, :, :dh], cb_ref[c, :, :dh]], axis=0)
        hsum = _dot(lhs, rhs)
        rows = slice(c * ln, (c + 1) * ln)
        y = _sigmoid(og_ref[rows, hs].astype(F32)) * (_rms(hsum, ng) + skip * uc_ref[rows, hs].astype(F32))
        y_ref[rows, hs] = y.astype(BF16)


def _mlstm(q, k, v, kc, vc, gl, gc, og, uc, ng, skip):
    bsz, t, wd = q.shape
    dh = MLSTM_HEAD_DIM
    nh = wd // dh
    nc = t // MLSTM_CHUNK
    n_ctx = kc.shape[1]
    assert n_ctx == MLSTM_CHUNK and gl.shape == (bsz, nh, 4, nc, MLSTM_CHUNK)
    hg = MLSTM_HEADS_PER_STEP
    assert nh % hg == 0
    seq = lambda n: pl.BlockSpec((None, n, hg * dh), lambda b, h: (b, 0, h))
    head_vec = pl.BlockSpec((1, hg * dh), lambda b, h: (0, h))
    state = pltpu.VMEM((nc, dh, dh + LANES), BF16)
    return pl.pallas_call(
        functools.partial(_mlstm_kernel, nc=nc, heads=hg),
        out_shape=jax.ShapeDtypeStruct((bsz, t, wd), BF16),
        grid=(bsz, nh // hg),
        in_specs=[seq(t), seq(t), seq(t), seq(n_ctx), seq(n_ctx),
                  pl.BlockSpec((None, hg, 4, nc, MLSTM_CHUNK), lambda b, h: (b, h, 0, 0, 0)),
                  pl.BlockSpec((None, hg, 4, 1, MLSTM_CHUNK), lambda b, h: (b, h, 0, 0, 0)),
                  seq(t), seq(t), head_vec, head_vec],
        out_specs=seq(t),
        scratch_shapes=[state] * (2 * hg),
        compiler_params=_params("parallel", "parallel"),
        name="mlstm",
    )(q, k, v, kc, vc, gl, gc, og, uc, ng, skip)


def _out1_kernel(x_ref, y_ref, wo_ref, g_ref, mod_ref, o_ref):
    y = _dot(y_ref[...], wo_ref[...])
    o_ref[...] = x_ref[...] + mod_ref[2:3, :] * _rms(y, g_ref[...])


def _out1(x, y, w_out, g, mod):
    bsz, t, d = x.shape
    tm = min(t, LIGHT_ROWS)
    wd = y.shape[2]
    tile = lambda n: pl.BlockSpec((None, tm, n), lambda b, i: (b, i, 0))
    return pl.pallas_call(
        _out1_kernel,
        out_shape=jax.ShapeDtypeStruct((bsz, t, d), F32),
        grid=(bsz, t // tm),
        in_specs=[tile(d), tile(wd), _const_spec(w_out.shape), _const_spec(g.shape),
                  pl.BlockSpec((None, 6, d), lambda b, i: (b, 0, 0))],
        out_specs=tile(d),
        compiler_params=_params("parallel", "parallel"),
        name="out1",
    )(x, y, w_out, g, mod)


def _rope_tables(t_len, scale):
    rows = t_len // GRID_W
    row = jnp.repeat(jnp.arange(rows, dtype=F32), GRID_W)
    col = jnp.tile(jnp.arange(GRID_W, dtype=F32), rows)
    n_freq = HEAD_DIM // 4
    inv = ROPE_BASE ** (-jnp.arange(n_freq, dtype=F32) / n_freq)
    ar, ac = row[:, None] * inv, col[:, None] * inv
    cos = jnp.concatenate([jnp.cos(ar), jnp.cos(ar), jnp.cos(ac), jnp.cos(ac)], axis=-1)
    sin = jnp.concatenate([-jnp.sin(ar), jnp.sin(ar), -jnp.sin(ac), jnp.sin(ac)], axis=-1)
    reps = LANES // HEAD_DIM
    return jnp.tile(cos, (1, reps)) * scale, jnp.tile(sin, (1, reps)) * scale


def _ffn_weights(w_up, conv_w, conv_b, w_down):
    return w_up.astype(BF16), conv_w, conv_b.reshape(1, -1), w_down.astype(BF16)


def kernel(x, c, ctx, c_ctx, mod_w, mod_b, norm_g, attn_in_w, attn_sink, pool_w, pool_scale, attn_out_w,
           rec_in_w, rec_gate_b, rec_conv_w, rec_conv_b, rec_q_w, rec_k_w, rec_norm_g, rec_skip, rec_out_w,
           ffn_up_w, ffn_conv_w, ffn_conv_b, ffn_down_w):
    bsz, t, d = x.shape
    n_ctx = ctx.shape[1]
    ctx_row = bsz

    pad_rows = (-(bsz + 1)) % 16
    c_all = jnp.concatenate([c, c_ctx[None, :], jnp.zeros((pad_rows, d), F32)], axis=0)
    mod = _adaln(c_all, mod_w, mod_b)
    mod = mod.reshape(mod.shape[0], mod.shape[1], 6, d)
    row2 = lambda a: a.reshape(1, -1)

    w_in0 = attn_in_w[0].astype(BF16)
    qscale = HEAD_DIM ** -0.5
    tables = _rope_tables(t, qscale) + _rope_tables(t, 1.0)
    g00 = row2(norm_g[0, 0])
    q, k4, v4, u = _in0(x, mod[0], None, g00, w_in0, tables)
    qc, kc4, vc4, ucx = _in0(ctx, mod[0], ctx_row, g00, w_in0, None)
    pw = pool_w[0].astype(BF16)
    psc = row2(pool_scale[0])
    wo0 = attn_out_w[0].astype(BF16)
    g01 = row2(norm_g[0, 1])
    sink = attn_sink[0]
    x1 = _mix0(x, q, k4, v4, kc4, vc4, u, sink, pw, psc, wo0, g01, mod[0], None)
    c1 = _mix0(ctx, qc, None, None, kc4, vc4, ucx, sink, pw, psc, wo0, g01, mod[0], ctx_row)
    f0 = _ffn_weights(ffn_up_w[0], ffn_conv_w[0], ffn_conv_b[0], ffn_down_w[0])
    g02, g03 = row2(norm_g[0, 2]), row2(norm_g[0, 3])
    x2 = _ffn(x1, mod[0], None, g02, g03, *f0)
    c2 = _ffn(c1, mod[0], ctx_row, g02, g03, *f0)

    w = rec_in_w[0]
    wd = MLSTM_HEADS * MLSTM_HEAD_DIM
    ng = 4 * MLSTM_HEADS
    w_main = w[:, :3 * wd].astype(BF16)
    w_gate = jnp.concatenate([w[:, 3 * wd:], jnp.zeros((d, LANES - ng), F32)], axis=1).astype(BF16)
    gbias = jnp.concatenate([rec_gate_b[0].reshape(1, ng), jnp.zeros((1, LANES - ng), F32)], axis=1)
    cw, cb = rec_conv_w[0], row2(rec_conv_b[0])
    qw = rec_q_w[0].astype(BF16)
    kw = (rec_k_w[0] * (MLSTM_HEAD_DIM ** -0.5)).astype(BF16)
    g10 = row2(norm_g[1, 0])
    uc, og, v, qm, km, gts = _in1(x2, mod[1], None, g10, w_main, w_gate, gbias, cw, cb, qw, kw, True)
    vcx, kcx, gtc = _in1(c2, mod[1], ctx_row, g10, w_main, w_gate, gbias, cw, cb, None, kw, False)

    def gate_rows(gt, n):
        gt = gt[:, :, :ng].reshape(bsz, n // MLSTM_CHUNK, MLSTM_CHUNK, 4, MLSTM_HEADS)
        return gt.transpose(0, 4, 3, 1, 2)

    y = _mlstm(qm, km, v, kcx, vcx, gate_rows(gts, t), gate_rows(gtc, n_ctx), og, uc,
               row2(rec_norm_g[0]), row2(rec_skip[0]))
    x3 = _out1(x2, y, rec_out_w[0].astype(BF16), row2(norm_g[1, 1]), mod[1])
    f1 = _ffn_weights(ffn_up_w[1], ffn_conv_w[1], ffn_conv_b[1], ffn_down_w[1])
    return _ffn(x3, mod[1], None, row2(norm_g[1, 2]), row2(norm_g[1, 3]), *f1)
```

```python
import functools

import jax
import jax.numpy as jnp
from jax import lax
from jax.experimental import pallas as pl
from jax.experimental.pallas import tpu as pltpu

F32 = jnp.float32
BF16 = jnp.bfloat16

GRID_W = 64
HEAD_DIM = 64
ATTN_Q_HEADS = 8
ATTN_KV_HEADS = 2
ATTN_BLOCK = 128
ROPE_BASE = 10000.0
POOL_WINDOWS = (2, 4, 8, 16)
POOL_CH = 128
MLSTM_HEADS = 4
MLSTM_HEAD_DIM = 256
MLSTM_CHUNK = 256
MLSTM_HEADS_PER_STEP = 2
EPS = 1e-6
NEG = -1e30

LANES = 128
SUBLANES = 8
VMEM_LIMIT = 56 * 1024 * 1024


def _params(*sem):
    return pltpu.CompilerParams(dimension_semantics=sem, vmem_limit_bytes=VMEM_LIMIT)


def _rms(x, g):
    return x * lax.rsqrt(jnp.mean(x * x, axis=-1, keepdims=True) + EPS) * g


def _sigmoid(x):
    return 1.0 / (1.0 + jnp.exp(-x))


def _dot(a, b):
    return jnp.dot(a, b, preferred_element_type=F32)


def _dot_nt(a, b):
    return lax.dot_general(a, b, (((1,), (1,)), ((), ())), preferred_element_type=F32)


def _dot_tn(a, b):
    return lax.dot_general(a, b, (((0,), (0,)), ((), ())), preferred_element_type=F32)


def _const_spec(shape):
    nd = len(shape)
    return pl.BlockSpec(shape, lambda *_: (0,) * nd)


def _adaln_kernel(c_ref, w_ref, b_ref, o_ref):
    c = c_ref[...]
    s = c * _sigmoid(c)
    o_ref[...] = _dot(s.astype(BF16), w_ref[...].astype(BF16)) + b_ref[...]


def _adaln(c_all, mod_w, mod_b):
    depth, d, n6 = mod_w.shape
    rows = c_all.shape[0]
    tn = 1536
    return pl.pallas_call(
        _adaln_kernel,
        out_shape=jax.ShapeDtypeStruct((depth, rows, n6), F32),
        grid=(depth, n6 // tn),
        in_specs=[
            pl.BlockSpec((rows, d), lambda l, j: (0, 0)),
            pl.BlockSpec((None, d, tn), lambda l, j: (l, 0, j)),
            pl.BlockSpec((None, 1, tn), lambda l, j: (l, 0, j)),
        ],
        out_specs=pl.BlockSpec((None, rows, tn), lambda l, j: (l, 0, j)),
        compiler_params=_params("arbitrary", "arbitrary"),
        name="adaln",
    )(c_all, mod_w, mod_b.reshape(depth, 1, n6))


def _in0_kernel(*refs, rope):
    if rope:
        x_ref, mod_ref, g_ref, w_ref, cq_ref, sq_ref, ck_ref, sk_ref, q_ref, k_ref, v_ref, u_ref = refs
    else:
        x_ref, mod_ref, g_ref, w_ref, q_ref, k_ref, v_ref, u_ref = refs
    x = x_ref[...]
    h = _rms(x, g_ref[...]) * (1.0 + mod_ref[1:2, :]) + mod_ref[0:1, :]
    y = _dot(h.astype(BF16), w_ref[...])
    nq = q_ref.shape[1]
    nkv = ATTN_KV_HEADS * HEAD_DIM
    tm = x.shape[0]
    lane = lax.broadcasted_iota(jnp.int32, (tm, LANES), 1)
    first = lane < HEAD_DIM

    def dup(a):
        sw = pltpu.roll(a, HEAD_DIM, 1)
        return jnp.concatenate([jnp.where(first, a, sw), jnp.where(first, sw, a)], axis=1)

    def put_keys(kf):
        for jb in range(tm // ATTN_BLOCK):
            k_ref[jb] = jnp.transpose(kf[jb * ATTN_BLOCK:(jb + 1) * ATTN_BLOCK, :]).astype(BF16)

    if rope:
        even = (lane // 16) % 2 == 0

        def rot(a, c, s):
            sw = jnp.where(even, pltpu.roll(a, LANES - 16, 1), pltpu.roll(a, 16, 1))
            return a * c + sw * s

        cq, sq, ck, sk = cq_ref[...], sq_ref[...], ck_ref[...], sk_ref[...]
        for j in range(nq // LANES):
            q_ref[:, j * LANES:(j + 1) * LANES] = rot(y[:, j * LANES:(j + 1) * LANES], cq, sq).astype(BF16)
        put_keys(dup(rot(y[:, nq:nq + nkv], ck, sk)))
    else:
        q_ref[...] = (y[:, :nq] * (HEAD_DIM ** -0.5)).astype(BF16)
        put_keys(dup(y[:, nq:nq + nkv]))
    v_ref[...] = dup(y[:, nq + nkv:nq + 2 * nkv]).astype(BF16)
    u_ref[...] = y[:, nq + 2 * nkv:].astype(BF16)


def _in0(x, mod, mod_row, g, w, tables):
    bsz, t, d = x.shape
    tm = min(t, LIGHT_ROWS)
    rope = tables is not None
    nq, nk, nu = 512, 256, 512
    row = (lambda b, i: (b, 0, 0)) if mod_row is None else (lambda b, i: (mod_row, 0, 0))
    in_specs = [
        pl.BlockSpec((None, tm, d), lambda b, i: (b, i, 0)),
        pl.BlockSpec((None, 6, d), row),
        _const_spec((1, d)),
        _const_spec(w.shape),
    ]
    args = [x, mod, g, w]
    if rope:
        in_specs += [pl.BlockSpec((tm, LANES), lambda b, i: (i, 0))] * 4
        args += list(tables)
    rows = lambda n: (jax.ShapeDtypeStruct((bsz, t, n), BF16), pl.BlockSpec((None, tm, n), lambda b, i: (b, i, 0)))
    keys_t = (jax.ShapeDtypeStruct((bsz, t // ATTN_BLOCK, nk, ATTN_BLOCK), BF16),
              pl.BlockSpec((None, tm // ATTN_BLOCK, nk, ATTN_BLOCK), lambda b, i: (b, i, 0, 0)))
    outs = [rows(nq), keys_t, rows(nk), rows(nu)]
    return pl.pallas_call(
        functools.partial(_in0_kernel, rope=rope),
        out_shape=[o[0] for o in outs],
        grid=(bsz, t // tm),
        in_specs=in_specs,
        out_specs=[o[1] for o in outs],
        compiler_params=_params("parallel", "parallel"),
        name="in0_rope" if rope else "in0_ctx",
    )(*args)


def _mix0_kernel(*refs, band, t_len, tq):
    if band:
        (sink_ref, x_ref, q_ref, k_ref, v_ref, kc_ref, vc_ref, u_ref, pw_ref, ps_ref, wo_ref, g_ref, mod_ref,
         o_ref, cat_ref) = refs
    else:
        (sink_ref, x_ref, q_ref, kc_ref, vc_ref, u_ref, pw_ref, ps_ref, wo_ref, g_ref, mod_ref,
         o_ref, cat_ref) = refs
    blk = ATTN_BLOCK
    nb = t_len // blk
    nsub = tq // blk
    tstep = pl.program_id(1)
    n_ctx = vc_ref.shape[0]
    n_keys = (3 * blk if band else 0) + n_ctx

    qrow = jnp.bitwise_and(lax.broadcasted_iota(jnp.int32, (2 * blk, blk), 0), blk - 1)
    kcol = lax.broadcasted_iota(jnp.int32, (2 * blk, blk), 1)
    tri_prev = kcol >= qrow
    tri_next = kcol <= qrow
    top_rows = lax.broadcasted_iota(jnp.int32, (2 * blk, 1), 0) < blk
    dim_lo = lax.broadcasted_iota(jnp.int32, (LANES, n_keys), 0) < HEAD_DIM
    lane_lo = lax.broadcasted_iota(jnp.int32, (n_keys, LANES), 1) < HEAD_DIM
    lane_o = lax.broadcasted_iota(jnp.int32, (2 * blk, LANES), 1) < HEAD_DIM
    zero = jnp.zeros((), BF16)
    ones_lo = jnp.where(lane_lo, 1.0, 0.0).astype(BF16)
    ones_hi = jnp.where(lane_lo, 0.0, 1.0).astype(BF16)
    row3 = lax.broadcasted_iota(jnp.int32, (blk, 3 * blk), 0)
    col3 = lax.broadcasted_iota(jnp.int32, (blk, 3 * blk), 1) - blk
    pool_band = [((col3 >= row3 - w // 2) & (col3 <= row3 + w // 2 - 1)).astype(BF16) for w in POOL_WINDOWS]
    row_pos = lax.broadcasted_iota(jnp.int32, (blk, LANES), 0)

    for j in range(nsub):
        n = tstep * nsub + j
        pblk = jnp.maximum(n - 1, 0)
        nblk = jnp.minimum(n + 1, nb - 1)
        r0 = pl.multiple_of(n * blk, blk)
        ps = pl.multiple_of(pblk * blk, blk)
        ns = pl.multiple_of(nblk * blk, blk)
        has_prev = n > 0
        has_next = n < nb - 1
        rows = slice(j * blk, (j + 1) * blk)

        if band:
            vb = jnp.concatenate([v_ref[pl.ds(ps, blk), :], v_ref[pl.ds(r0, blk), :], v_ref[pl.ds(ns, blk), :],
                                  vc_ref[...]], axis=0)
        else:
            vb = vc_ref[...]
        for hk in range(ATTN_KV_HEADS):
            hd = slice(hk * LANES, (hk + 1) * LANES)
            kt = [kc_ref[i, hd, :] for i in range(kc_ref.shape[0])]
            if band:
                kt = [k_ref[pblk, hd, :], k_ref[n, hd, :], k_ref[nblk, hd, :]] + kt
            kt = jnp.concatenate(kt, axis=1)
            vv = vb[:, hd]
            v_aug = jnp.concatenate(
                [jnp.concatenate([jnp.where(lane_lo, vv, zero), ones_lo], axis=1),
                 jnp.concatenate([jnp.where(lane_lo, zero, vv), ones_hi], axis=1)], axis=0)
            c0, c1 = 2 * hk, 2 * hk + 1
            q2 = jnp.concatenate([q_ref[rows, c0 * LANES:(c0 + 1) * LANES],
                                  q_ref[rows, c1 * LANES:(c1 + 1) * LANES]], axis=0)
            probs, esink = [], []
            s_both = _dot(q2, jnp.concatenate([jnp.where(dim_lo, kt, zero), jnp.where(dim_lo, zero, kt)], axis=1))
            for half in range(2):
                s = s_both[:, half * n_keys:(half + 1) * n_keys]
                if band:
                    s = jnp.concatenate([jnp.where(tri_prev & has_prev, s[:, :blk], NEG), s[:, blk:2 * blk],
                                         jnp.where(tri_next & has_next, s[:, 2 * blk:3 * blk], NEG),
                                         s[:, 3 * blk:]], axis=1)
                snk = jnp.where(top_rows, sink_ref[4 * hk + half], sink_ref[4 * hk + 2 + half])
                m = jnp.maximum(jnp.max(s, axis=-1, keepdims=True), snk)
                probs.append(jnp.exp(s - m).astype(BF16))
                esink.append(jnp.exp(snk - m))
            o2 = _dot(jnp.concatenate(probs, axis=1), v_aug)
            o = o2[:, :LANES] / (o2[:, LANES:] + jnp.where(lane_o, esink[0], esink[1]))
            cat_ref[rows, c0 * LANES:(c0 + 1) * LANES] = o[:blk].astype(BF16)
            cat_ref[rows, c1 * LANES:(c1 + 1) * LANES] = o[blk:].astype(BF16)

        uo = u_ref[pl.ds(r0, blk), :]
        up = jnp.where(has_prev, u_ref[pl.ds(ps, blk), :], zero)
        un = jnp.where(has_next, u_ref[pl.ds(ns, blk), :], zero)
        ub = jnp.concatenate([up, uo, un], axis=0)
        pos = r0 + row_pos
        for gi, w in enumerate(POOL_WINDOWS):
            cs = slice(gi * POOL_CH, (gi + 1) * POOL_CH)
            sums = _dot(pool_band[gi], ub[:, cs])
            cnt = jnp.minimum(pos + w // 2, t_len) - jnp.maximum(pos - w // 2, 0)
            dlt = sums / cnt.astype(F32) - uo[:, cs].astype(F32)
            y = _dot(dlt.astype(BF16), pw_ref[gi]) * ps_ref[:, cs]
            off = ATTN_Q_HEADS * HEAD_DIM + gi * POOL_CH
            cat_ref[rows, off:off + POOL_CH] = y.astype(BF16)

    y = _dot(cat_ref[...], wo_ref[...])
    o_ref[...] = x_ref[...] + mod_ref[2:3, :] * _rms(y, g_ref[...])


def _mix0(x, q, k4, v4, kc4, vc4, u, sink, pool_w, pool_scale, w_out, g, mod, mod_row):
    bsz, t, d = x.shape
    band = k4 is not None
    tq = min(t, LIGHT_ROWS)
    n_ctx = vc4.shape[1]
    row = (lambda b, i: (b, 0, 0)) if mod_row is None else (lambda b, i: (mod_row, 0, 0))
    full = lambda n, w: pl.BlockSpec((None, n, w), lambda b, i: (b, 0, 0))
    keys_t = lambda a: pl.BlockSpec((None,) + a.shape[1:], lambda b, i: (b, 0, 0, 0))
    in_specs = [
        pl.BlockSpec(memory_space=pltpu.SMEM),
        pl.BlockSpec((None, tq, d), lambda b, i: (b, i, 0)),
        pl.BlockSpec((None, tq, q.shape[2]), lambda b, i: (b, i, 0)),
    ]
    args = [sink, x, q]
    if band:
        in_specs += [keys_t(k4), full(t, v4.shape[2])]
        args += [k4, v4]
    in_specs += [keys_t(kc4), full(n_ctx, vc4.shape[2]), full(t, u.shape[2]),
                 _const_spec(pool_w.shape), _const_spec(pool_scale.shape), _const_spec(w_out.shape),
                 _const_spec(g.shape), pl.BlockSpec((None, 6, d), row)]
    args += [kc4, vc4, u, pool_w, pool_scale, w_out, g, mod]
    return pl.pallas_call(
        functools.partial(_mix0_kernel, band=band, t_len=t, tq=tq),
        out_shape=jax.ShapeDtypeStruct((bsz, t, d), F32),
        grid=(bsz, t // tq),
        in_specs=in_specs,
        out_specs=pl.BlockSpec((None, tq, d), lambda b, i: (b, i, 0)),
        scratch_shapes=[pltpu.VMEM((tq, w_out.shape[0]), BF16)],
        compiler_params=_params("parallel", "parallel"),
        name="mix0_band" if band else "mix0_ctx",
    )(*args)


HALO = SUBLANES
FFN_CHUNK = 256
FFN_ROWS = 512
LIGHT_ROWS = 1024


def _halo_specs(tm, t, d):
    nblk = tm // HALO
    last = t // HALO - 1
    return [
        pl.BlockSpec((None, tm, d), lambda b, i: (b, i, 0)),
        pl.BlockSpec((None, HALO, d), lambda b, i: (b, jnp.maximum(i * nblk - 1, 0), 0)),
        pl.BlockSpec((None, HALO, d), lambda b, i: (b, jnp.minimum((i + 1) * nblk, last), 0)),
    ]


def _conv3(u, cw, cb, tm):
    return (cw[0:1, :] * u[HALO - 1:HALO - 1 + tm, :] + cw[1:2, :] * u[HALO:HALO + tm, :]
            + cw[2:3, :] * u[HALO + 1:HALO + 1 + tm, :] + cb)


def _ffn_kernel(xm_ref, xp_ref, xn_ref, mod_ref, gpre_ref, gpost_ref, wup_ref, cw_ref, cb_ref, wdn_ref,
                o_ref, hb_ref, acc_ref, u0_ref, u1_ref, *, tm, nt):
    i = pl.program_id(1)
    shift, scale, gate = mod_ref[3:4, :], mod_ref[4:5, :], mod_ref[5:6, :]
    gpre = gpre_ref[...]

    def pre(xv):
        return _rms(xv, gpre) * (1.0 + scale) + shift

    hp = jnp.where(i > 0, pre(xp_ref[...]), 0.0)
    hn = jnp.where(i < nt - 1, pre(xn_ref[...]), 0.0)
    hb_ref[...] = jnp.concatenate([hp, pre(xm_ref[...]), hn], axis=0).astype(BF16)
    acc_ref[...] = jnp.zeros_like(acc_ref)
    dff = wdn_ref.shape[0]
    nchunk = dff // FFN_CHUNK
    half = FFN_CHUNK // LANES
    u_refs = (u0_ref, u1_ref)

    def cols(c, j):
        o = (dff if j >= half else 0) + c * FFN_CHUNK + (j % half) * LANES
        return slice(o, o + LANES)

    def up(c, s):
        for part in range(2):
            o = part * dff + c * FFN_CHUNK
            u = _dot(hb_ref[...], wup_ref[:, o:o + FFN_CHUNK])
            for j in range(half):
                u_refs[s][part * half + j] = u[:, j * LANES:(j + 1) * LANES]

    def conv_act(c, s):
        ur = u_refs[s]

        def conv(j):
            cs = cols(c, j)
            return (cw_ref[0:1, cs] * ur[j, pl.ds(HALO - 1, tm), :] + cw_ref[1:2, cs] * ur[j, pl.ds(HALO, tm), :]
                    + cw_ref[2:3, cs] * ur[j, pl.ds(HALO + 1, tm), :] + cb_ref[:, cs])

        acts = []
        for j in range(half):
            ag, av = conv(j), conv(half + j)
            acts.append((ag * _sigmoid(ag) * av).astype(BF16))
        return jnp.concatenate(acts, axis=1)

    def down(c, act):
        acc_ref[...] += _dot(act, wdn_ref[c * FFN_CHUNK:(c + 1) * FFN_CHUNK, :])

    up(0, 0)
    prev = None
    for c in range(nchunk):
        if c + 1 < nchunk:
            up(c + 1, (c + 1) % 2)
        if prev is not None:
            down(c - 1, prev)
        prev = conv_act(c, c % 2)
    down(nchunk - 1, prev)
    o_ref[...] = xm_ref[...] + gate * _rms(acc_ref[...], gpost_ref[...])


def _ffn(x, mod, mod_row, gpre, gpost, wup, cw, cb, wdn):
    bsz, t, d = x.shape
    tm = min(t, FFN_ROWS)
    nt = t // tm
    assert wdn.shape[0] % FFN_CHUNK == 0
    row = (lambda b, i: (b, 0, 0)) if mod_row is None else (lambda b, i: (mod_row, 0, 0))
    in_specs = _halo_specs(tm, t, d) + [
        pl.BlockSpec((None, 6, d), row), _const_spec(gpre.shape), _const_spec(gpost.shape),
        _const_spec(wup.shape), _const_spec(cw.shape), _const_spec(cb.shape), _const_spec(wdn.shape)]
    u_buf = pltpu.VMEM((2 * FFN_CHUNK // LANES, tm + 2 * HALO, LANES), F32)
    return pl.pallas_call(
        functools.partial(_ffn_kernel, tm=tm, nt=nt),
        out_shape=jax.ShapeDtypeStruct((bsz, t, d), F32),
        grid=(bsz, nt),
        in_specs=in_specs,
        out_specs=pl.BlockSpec((None, tm, d), lambda b, i: (b, i, 0)),
        scratch_shapes=[pltpu.VMEM((tm + 2 * HALO, d), BF16), pltpu.VMEM((tm, d), F32), u_buf, u_buf],
        compiler_params=_params("parallel", "parallel"),
        name="ffn",
    )(x, x, x, mod, gpre, gpost, wup, cw, cb, wdn)


def _in1_kernel(*refs, tm, nt, full):
    if full:
        (xm_ref, xp_ref, xn_ref, mod_ref, g_ref, w_ref, wg_ref, gb_ref, cw_ref, cb_ref, qw_ref, kw_ref,
         uc_ref, og_ref, v_ref, q_ref, k_ref, gt_ref) = refs
    else:
        (xm_ref, xp_ref, xn_ref, mod_ref, g_ref, w_ref, wg_ref, gb_ref, cw_ref, cb_ref, kw_ref,
         v_ref, k_ref, gt_ref) = refs
    i = pl.program_id(1)
    shift, scale = mod_ref[0:1, :], mod_ref[1:2, :]
    g = g_ref[...]

    def pre(xv):
        return _rms(xv, g) * (1.0 + scale) + shift

    hm = pre(xm_ref[...])
    hp = jnp.where(i > 0, pre(xp_ref[...]), 0.0)
    hn = jnp.where(i < nt - 1, pre(xn_ref[...]), 0.0)
    hext = jnp.concatenate([hp, hm, hn], axis=0).astype(BF16)
    dv = v_ref.shape[1]
    u = _dot(hext, w_ref[:, :dv])
    uc = _conv3(u, cw_ref[...], cb_ref[...], tm)
    uc = uc * _sigmoid(uc)
    ucb = uc.astype(BF16)
    hmb = hm.astype(BF16)
    v_ref[...] = _dot(hmb, w_ref[:, dv:2 * dv]).astype(BF16)
    gt_ref[...] = _dot(hmb, wg_ref[...]) + gb_ref[...]
    if full:
        og_ref[...] = _dot(hmb, w_ref[:, 2 * dv:3 * dv]).astype(BF16)
        uc_ref[...] = ucb
    dh = MLSTM_HEAD_DIM
    for hh in range(MLSTM_HEADS):
        uh = ucb[:, hh * dh:(hh + 1) * dh]
        k_ref[:, hh * dh:(hh + 1) * dh] = _dot(uh, kw_ref[hh]).astype(BF16)
        if full:
            q_ref[:, hh * dh:(hh + 1) * dh] = _dot(uh, qw_ref[hh]).astype(BF16)


def _in1(x, mod, mod_row, g, w, wg, gb, cw, cb, qw, kw, full):
    bsz, t, d = x.shape
    tm = min(t, 512)
    nt = t // tm
    dw = w.shape[1] // 3
    row = (lambda b, i: (b, 0, 0)) if mod_row is None else (lambda b, i: (mod_row, 0, 0))
    in_specs = _halo_specs(tm, t, d) + [pl.BlockSpec((None, 6, d), row)]
    consts = [g, w, wg, gb, cw, cb] + ([qw] if full else []) + [kw]
    in_specs += [_const_spec(a.shape) for a in consts]
    act = lambda: jax.ShapeDtypeStruct((bsz, t, dw), BF16)
    gates = jax.ShapeDtypeStruct((bsz, t, LANES), F32)
    out_shape = [act(), act(), act(), act(), act(), gates] if full else [act(), act(), gates]
    out_specs = [pl.BlockSpec((None, tm, s.shape[2]), lambda b, i: (b, i, 0)) for s in out_shape]
    return pl.pallas_call(
        functools.partial(_in1_kernel, tm=tm, nt=nt, full=full),
        out_shape=out_shape,
        grid=(bsz, nt),
        in_specs=in_specs,
        out_specs=out_specs,
        compiler_params=_params("parallel", "parallel"),
        name="in1_full" if full else "in1_ctx",
    )(x, x, x, mod, *consts)


def _log_sigmoid(x):
    return jnp.minimum(x, 0.0) - jnp.log1p(jnp.exp(-jnp.abs(x)))


def _scan_lanes(x, op, fill, reverse):
    n = x.shape[1]
    lane = lax.broadcasted_iota(jnp.int32, x.shape, 1)
    d = 1
    while d < n:
        if reverse:
            shifted = jnp.where(lane < n - d, pltpu.roll(x, n - d, 1), fill)
        else:
            shifted = jnp.where(lane >= d, pltpu.roll(x, d, 1), fill)
        x = op(x, shifted)
        d *= 2
    return x


def _mlstm_kernel(*refs, nc, heads):
    io, states = refs[:12], refs[12:]
    for hh in range(heads):
        _mlstm_head(hh, *io, states[2 * hh], states[2 * hh + 1], nc=nc)


def _mlstm_head(hh, q_ref, k_ref, v_ref, kc_ref, vc_ref, gl_ref, gc_ref, og_ref, uc_ref, ng_ref, skip_ref,
                y_ref, cf_ref, cb_ref, *, nc):
    ln = MLSTM_CHUNK
    dh = MLSTM_HEAD_DIM
    hs = slice(hh * dh, (hh + 1) * dh)
    nr = nc + 1
    scores = [_dot_nt(q_ref[c * ln:(c + 1) * ln, hs], k_ref[c * ln:(c + 1) * ln, hs]) for c in range(nc)]
    gates = [jnp.concatenate([gc_ref[hh, gi], gl_ref[hh, gi]], axis=0) for gi in range(4)]

    def direction(ig, fpre, reverse):
        b = _scan_lanes(_log_sigmoid(fpre), jnp.add, 0.0, reverse)
        a = ig - b
        cmax = _scan_lanes(a, jnp.maximum, NEG, reverse)
        e = 0 if reverse else ln - 1
        b_end, a_max = b[:, e:e + 1], cmax[:, e:e + 1]
        order = [0] + (list(range(nc, 0, -1)) if reverse else list(range(1, nr)))
        m_in = [None] * nr
        m = jnp.zeros((1, 1), F32)
        for r in order:
            m_in[r] = m
            m = b_end[r:r + 1] + jnp.maximum(m, a_max[r:r + 1])
        m_in = jnp.concatenate(m_in, axis=0)
        big_m = jnp.maximum(m_in, cmax)
        m_end = jnp.maximum(m_in, a_max)
        return dict(a=a, M=big_m, mt=b + big_m, m_in=m_in, wts=jnp.exp(a - m_end), decay=jnp.exp(m_in - m_end))

    fw = direction(gates[0], gates[1], False)
    bw = direction(gates[2], gates[3], True)

    def lane_bcast(row):
        return jnp.transpose(jnp.broadcast_to(row, (LANES, ln)))

    def wide(a):
        return jnp.concatenate([a] * (dh // LANES), axis=1)

    ones = jnp.ones((ln, LANES), BF16)

    def chunk_kv(r):
        if r == 0:
            return kc_ref[:, hs], vc_ref[:, hs]
        return k_ref[(r - 1) * ln:r * ln, hs], v_ref[(r - 1) * ln:r * ln, hs]

    contrib_f, contrib_b = [], []
    for r in range(nr):
        kk, vv = chunk_kv(r)
        kt = jnp.transpose(kk.astype(F32))
        v_aug = jnp.concatenate([vv, ones], axis=1)
        contrib_f.append(_dot((kt * fw["wts"][r:r + 1]).astype(BF16), v_aug))
        contrib_b.append(_dot((kt * bw["wts"][r:r + 1]).astype(BF16), v_aug))

    st = contrib_f[0]
    for c in range(nc):
        cf_ref[c] = st.astype(BF16)
        if c + 1 < nc:
            st = fw["decay"][c + 1:c + 2] * st + contrib_f[c + 1]
    st = contrib_b[0]
    for c in range(nc - 1, -1, -1):
        cb_ref[c] = st.astype(BF16)
        if c > 0:
            st = bw["decay"][c + 1:c + 2] * st + contrib_b[c + 1]

    ti = lax.broadcasted_iota(jnp.int32, (ln, ln), 0)
    sj = lax.broadcasted_iota(jnp.int32, (ln, ln), 1)
    ng = ng_ref[:, hs]
    skip = skip_ref[:, hs]
    for c in range(nc):
        r = c + 1
        qc = q_ref[c * ln:(c + 1) * ln, hs]
        kk, vv = chunk_kv(r)
        s = scores[c]
        qf = qc.astype(F32)
        lhs = []
        p_sum = None
        for dirn, causal, st_ref in ((fw, sj <= ti, cf_ref), (bw, sj >= ti, cb_ref)):
            big_m = lane_bcast(dirn["M"][r:r + 1])
            p = s * jnp.exp(jnp.where(causal, dirn["a"][r:r + 1] - wide(big_m), NEG))
            iw = jnp.exp(dirn["m_in"][r:r + 1] - big_m)
            den = _dot(p.astype(BF16), ones) + iw * _dot(qc, st_ref[c, :, dh:])
            rinv = 1.0 / jnp.maximum(jnp.abs(den), jnp.exp(-lane_bcast(dirn["mt"][r:r + 1])))
            p_sum = p * wide(rinv) if p_sum is None else p_sum + p * wide(rinv)
            lhs.append((qf * wide(iw * rinv)).astype(BF16))
        lhs = jnp.concatenate([p_sum.astype(BF16)] + lhs, axis=1)
        rhs = jnp.concatenate([vv, cf_ref[c, :, :dh], cb_ref[c, :, :dh]], axis=0)
        hsum = _dot(lhs, rhs)
        rows = slice(c * ln, (c + 1) * ln)
        y = _sigmoid(og_ref[rows, hs].astype(F32)) * (_rms(hsum, ng) + skip * uc_ref[rows, hs].astype(F32))
        y_ref[rows, hs] = y.astype(BF16)


def _mlstm(q, k, v, kc, vc, gl, gc, og, uc, ng, skip):
    bsz, t, wd = q.shape
    dh = MLSTM_HEAD_DIM
    nh = wd // dh
    nc = t // MLSTM_CHUNK
    n_ctx = kc.shape[1]
    assert n_ctx == MLSTM_CHUNK and gl.shape == (bsz, nh, 4, nc, MLSTM_CHUNK)
    hg = MLSTM_HEADS_PER_STEP
    assert nh % hg == 0
    seq = lambda n: pl.BlockSpec((None, n, hg * dh), lambda b, h: (b, 0, h))
    head_vec = pl.BlockSpec((1, hg * dh), lambda b, h: (0, h))
    state = pltpu.VMEM((nc, dh, dh + LANES), BF16)
    return pl.pallas_call(
        functools.partial(_mlstm_kernel, nc=nc, heads=hg),
        out_shape=jax.ShapeDtypeStruct((bsz, t, wd), BF16),
        grid=(bsz, nh // hg),
        in_specs=[seq(t), seq(t), seq(t), seq(n_ctx), seq(n_ctx),
                  pl.BlockSpec((None, hg, 4, nc, MLSTM_CHUNK), lambda b, h: (b, h, 0, 0, 0)),
                  pl.BlockSpec((None, hg, 4, 1, MLSTM_CHUNK), lambda b, h: (b, h, 0, 0, 0)),
                  seq(t), seq(t), head_vec, head_vec],
        out_specs=seq(t),
        scratch_shapes=[state] * (2 * hg),
        compiler_params=_params("parallel", "parallel"),
        name="mlstm",
    )(q, k, v, kc, vc, gl, gc, og, uc, ng, skip)


def _out1_kernel(x_ref, y_ref, wo_ref, g_ref, mod_ref, o_ref):
    y = _dot(y_ref[...], wo_ref[...])
    o_ref[...] = x_ref[...] + mod_ref[2:3, :] * _rms(y, g_ref[...])


def _out1(x, y, w_out, g, mod):
    bsz, t, d = x.shape
    tm = min(t, LIGHT_ROWS)
    wd = y.shape[2]
    tile = lambda n: pl.BlockSpec((None, tm, n), lambda b, i: (b, i, 0))
    return pl.pallas_call(
        _out1_kernel,
        out_shape=jax.ShapeDtypeStruct((bsz, t, d), F32),
        grid=(bsz, t // tm),
        in_specs=[tile(d), tile(wd), _const_spec(w_out.shape), _const_spec(g.shape),
                  pl.BlockSpec((None, 6, d), lambda b, i: (b, 0, 0))],
        out_specs=tile(d),
        compiler_params=_params("parallel", "parallel"),
        name="out1",
    )(x, y, w_out, g, mod)


def _rope_tables(t_len, scale):
    rows = t_len // GRID_W
    row = jnp.repeat(jnp.arange(rows, dtype=F32), GRID_W)
    col = jnp.tile(jnp.arange(GRID_W, dtype=F32), rows)
    n_freq = HEAD_DIM // 4
    inv = ROPE_BASE ** (-jnp.arange(n_freq, dtype=F32) / n_freq)
    ar, ac = row[:, None] * inv, col[:, None] * inv
    cos = jnp.concatenate([jnp.cos(ar), jnp.cos(ar), jnp.cos(ac), jnp.cos(ac)], axis=-1)
    sin = jnp.concatenate([-jnp.sin(ar), jnp.sin(ar), -jnp.sin(ac), jnp.sin(ac)], axis=-1)
    reps = LANES // HEAD_DIM
    return jnp.tile(cos, (1, reps)) * scale, jnp.tile(sin, (1, reps)) * scale


def _ffn_weights(w_up, conv_w, conv_b, w_down):
    return w_up.astype(BF16), conv_w, conv_b.reshape(1, -1), w_down.astype(BF16)


def kernel(x, c, ctx, c_ctx, mod_w, mod_b, norm_g, attn_in_w, attn_sink, pool_w, pool_scale, attn_out_w,
           rec_in_w, rec_gate_b, rec_conv_w, rec_conv_b, rec_q_w, rec_k_w, rec_norm_g, rec_skip, rec_out_w,
           ffn_up_w, ffn_conv_w, ffn_conv_b, ffn_down_w):
    bsz, t, d = x.shape
    n_ctx = ctx.shape[1]
    ctx_row = bsz

    pad_rows = (-(bsz + 1)) % 16
    c_all = jnp.concatenate([c, c_ctx[None, :], jnp.zeros((pad_rows, d), F32)], axis=0)
    mod = _adaln(c_all, mod_w, mod_b)
    mod = mod.reshape(mod.shape[0], mod.shape[1], 6, d)
    row2 = lambda a: a.reshape(1, -1)

    w_in0 = attn_in_w[0].astype(BF16)
    qscale = HEAD_DIM ** -0.5
    tables = _rope_tables(t, qscale) + _rope_tables(t, 1.0)
    g00 = row2(norm_g[0, 0])
    q, k4, v4, u = _in0(x, mod[0], None, g00, w_in0, tables)
    qc, kc4, vc4, ucx = _in0(ctx, mod[0], ctx_row, g00, w_in0, None)
    pw = pool_w[0].astype(BF16)
    psc = row2(pool_scale[0])
    wo0 = attn_out_w[0].astype(BF16)
    g01 = row2(norm_g[0, 1])
    sink = attn_sink[0]
    x1 = _mix0(x, q, k4, v4, kc4, vc4, u, sink, pw, psc, wo0, g01, mod[0], None)
    c1 = _mix0(ctx, qc, None, None, kc4, vc4, ucx, sink, pw, psc, wo0, g01, mod[0], ctx_row)
    f0 = _ffn_weights(ffn_up_w[0], ffn_conv_w[0], ffn_conv_b[0], ffn_down_w[0])
    g02, g03 = row2(norm_g[0, 2]), row2(norm_g[0, 3])
    x2 = _ffn(x1, mod[0], None, g02, g03, *f0)
    c2 = _ffn(c1, mod[0], ctx_row, g02, g03, *f0)

    w = rec_in_w[0]
    wd = MLSTM_HEADS * MLSTM_HEAD_DIM
    ng = 4 * MLSTM_HEADS
    w_main = w[:, :3 * wd].astype(BF16)
    w_gate = jnp.concatenate([w[:, 3 * wd:], jnp.zeros((d, LANES - ng), F32)], axis=1).astype(BF16)
    gbias = jnp.concatenate([rec_gate_b[0].reshape(1, ng), jnp.zeros((1, LANES - ng), F32)], axis=1)
    cw, cb = rec_conv_w[0], row2(rec_conv_b[0])
    qw = rec_q_w[0].astype(BF16)
    kw = (rec_k_w[0] * (MLSTM_HEAD_DIM ** -0.5)).astype(BF16)
    g10 = row2(norm_g[1, 0])
    uc, og, v, qm, km, gts = _in1(x2, mod[1], None, g10, w_main, w_gate, gbias, cw, cb, qw, kw, True)
    vcx, kcx, gtc = _in1(c2, mod[1], ctx_row, g10, w_main, w_gate, gbias, cw, cb, None, kw, False)

    def gate_rows(gt, n):
        gt = gt[:, :, :ng].reshape(bsz, n // MLSTM_CHUNK, MLSTM_CHUNK, 4, MLSTM_HEADS)
        return gt.transpose(0, 4, 3, 1, 2)

    y = _mlstm(qm, km, v, kcx, vcx, gate_rows(gts, t), gate_rows(gtc, n_ctx), og, uc,
               row2(rec_norm_g[0]), row2(rec_skip[0]))
    x3 = _out1(x2, y, rec_out_w[0].astype(BF16), row2(norm_g[1, 1]), mod[1])
    f1 = _ffn_weights(ffn_up_w[1], ffn_conv_w[1], ffn_conv_b[1], ffn_down_w[1])
    return _ffn(x3, mod[1], None, row2(norm_g[1, 2]), row2(norm_g[1, 3]), *f1)
```

```python
import functools

import jax
import jax.numpy as jnp
from jax import lax
from jax.experimental import pallas as pl
from jax.experimental.pallas import tpu as pltpu

F32 = jnp.float32
BF16 = jnp.bfloat16

GRID_W = 64
HEAD_DIM = 64
ATTN_Q_HEADS = 8
ATTN_KV_HEADS = 2
ATTN_BLOCK = 128
ROPE_BASE = 10000.0
POOL_WINDOWS = (2, 4, 8, 16)
POOL_CH = 128
MLSTM_HEADS = 4
MLSTM_HEAD_DIM = 256
MLSTM_CHUNK = 256
MLSTM_HEADS_PER_STEP = 2
EPS = 1e-6
NEG = -1e30

LANES = 128
SUBLANES = 8
VMEM_LIMIT = 56 * 1024 * 1024


def _params(*sem):
    return pltpu.CompilerParams(dimension_semantics=sem, vmem_limit_bytes=VMEM_LIMIT)


def _rms(x, g):
    return x * lax.rsqrt(jnp.mean(x * x, axis=-1, keepdims=True) + EPS) * g


def _sigmoid(x):
    return 1.0 / (1.0 + jnp.exp(-x))


def _dot(a, b):
    return jnp.dot(a, b, preferred_element_type=F32)


def _dot_nt(a, b):
    return lax.dot_general(a, b, (((1,), (1,)), ((), ())), preferred_element_type=F32)


def _dot_tn(a, b):
    return lax.dot_general(a, b, (((0,), (0,)), ((), ())), preferred_element_type=F32)


def _const_spec(shape):
    nd = len(shape)
    return pl.BlockSpec(shape, lambda *_: (0,) * nd)


def _adaln_kernel(c_ref, w_ref, b_ref, o_ref):
    c = c_ref[...]
    s = c * _sigmoid(c)
    o_ref[...] = _dot(s.astype(BF16), w_ref[...].astype(BF16)) + b_ref[...]


def _adaln(c_all, mod_w, mod_b):
    depth, d, n6 = mod_w.shape
    rows = c_all.shape[0]
    tn = 1536
    return pl.pallas_call(
        _adaln_kernel,
        out_shape=jax.ShapeDtypeStruct((depth, rows, n6), F32),
        grid=(depth, n6 // tn),
        in_specs=[
            pl.BlockSpec((rows, d), lambda l, j: (0, 0)),
            pl.BlockSpec((None, d, tn), lambda l, j: (l, 0, j)),
            pl.BlockSpec((None, 1, tn), lambda l, j: (l, 0, j)),
        ],
        out_specs=pl.BlockSpec((None, rows, tn), lambda l, j: (l, 0, j)),
        compiler_params=_params("arbitrary", "arbitrary"),
        name="adaln",
    )(c_all, mod_w, mod_b.reshape(depth, 1, n6))


def _in0_kernel(*refs, rope):
    if rope:
        x_ref, mod_ref, g_ref, w_ref, cq_ref, sq_ref, ck_ref, sk_ref, q_ref, k_ref, v_ref, u_ref = refs
    else:
        x_ref, mod_ref, g_ref, w_ref, q_ref, k_ref, v_ref, u_ref = refs
    x = x_ref[...]
    h = _rms(x, g_ref[...]) * (1.0 + mod_ref[1:2, :]) + mod_ref[0:1, :]
    y = _dot(h.astype(BF16), w_ref[...])
    nq = q_ref.shape[1]
    nkv = ATTN_KV_HEADS * HEAD_DIM
    tm = x.shape[0]
    lane = lax.broadcasted_iota(jnp.int32, (tm, LANES), 1)
    first = lane < HEAD_DIM

    def dup(a):
        sw = pltpu.roll(a, HEAD_DIM, 1)
        return jnp.concatenate([jnp.where(first, a, sw), jnp.where(first, sw, a)], axis=1)

    def put_keys(kf):
        for jb in range(tm // ATTN_BLOCK):
            k_ref[jb] = jnp.transpose(kf[jb * ATTN_BLOCK:(jb + 1) * ATTN_BLOCK, :]).astype(BF16)

    if rope:
        even = (lane // 16) % 2 == 0

        def rot(a, c, s):
            sw = jnp.where(even, pltpu.roll(a, LANES - 16, 1), pltpu.roll(a, 16, 1))
            return a * c + sw * s

        cq, sq, ck, sk = cq_ref[...], sq_ref[...], ck_ref[...], sk_ref[...]
        for j in range(nq // LANES):
            q_ref[:, j * LANES:(j + 1) * LANES] = rot(y[:, j * LANES:(j + 1) * LANES], cq, sq).astype(BF16)
        put_keys(dup(rot(y[:, nq:nq + nkv], ck, sk)))
    else:
        q_ref[...] = (y[:, :nq] * (HEAD_DIM ** -0.5)).astype(BF16)
        put_keys(dup(y[:, nq:nq + nkv]))
    v_ref[...] = dup(y[:, nq + nkv:nq + 2 * nkv]).astype(BF16)
    u_ref[...] = y[:, nq + 2 * nkv:].astype(BF16)


def _in0(x, mod, mod_row, g, w, tables):
    bsz, t, d = x.shape
    tm = min(t, LIGHT_ROWS)
    rope = tables is not None
    nq, nk, nu = 512, 256, 512
    row = (lambda b, i: (b, 0, 0)) if mod_row is None else (lambda b, i: (mod_row, 0, 0))
    in_specs = [
        pl.BlockSpec((None, tm, d), lambda b, i: (b, i, 0)),
        pl.BlockSpec((None, 6, d), row),
        _const_spec((1, d)),
        _const_spec(w.shape),
    ]
    args = [x, mod, g, w]
    if rope:
        in_specs += [pl.BlockSpec((tm, LANES), lambda b, i: (i, 0))] * 4
        args += list(tables)
    rows = lambda n: (jax.ShapeDtypeStruct((bsz, t, n), BF16), pl.BlockSpec((None, tm, n), lambda b, i: (b, i, 0)))
    keys_t = (jax.ShapeDtypeStruct((bsz, t // ATTN_BLOCK, nk, ATTN_BLOCK), BF16),
              pl.BlockSpec((None, tm // ATTN_BLOCK, nk, ATTN_BLOCK), lambda b, i: (b, i, 0, 0)))
    outs = [rows(nq), keys_t, rows(nk), rows(nu)]
    return pl.pallas_call(
        functools.partial(_in0_kernel, rope=rope),
        out_shape=[o[0] for o in outs],
        grid=(bsz, t // tm),
        in_specs=in_specs,
        out_specs=[o[1] for o in outs],
        compiler_params=_params("parallel", "parallel"),
        name="in0_rope" if rope else "in0_ctx",
    )(*args)


def _mix0_kernel(*refs, band, t_len, tq):
    if band:
        (sink_ref, x_ref, q_ref, k_ref, v_ref, kc_ref, vc_ref, u_ref, pw_ref, ps_ref, wo_ref, g_ref, mod_ref,
         o_ref, cat_ref) = refs
    else:
        (sink_ref, x_ref, q_ref, kc_ref, vc_ref, u_ref, pw_ref, ps_ref, wo_ref, g_ref, mod_ref,
         o_ref, cat_ref) = refs
    blk = ATTN_BLOCK
    nb = t_len // blk
    nsub = tq // blk
    tstep = pl.program_id(1)
    n_ctx = vc_ref.shape[0]
    n_keys = (3 * blk if band else 0) + n_ctx

    qrow = jnp.bitwise_and(lax.broadcasted_iota(jnp.int32, (2 * blk, blk), 0), blk - 1)
    kcol = lax.broadcasted_iota(jnp.int32, (2 * blk, blk), 1)
    tri_prev = kcol >= qrow
    tri_next = kcol <= qrow
    top_rows = lax.broadcasted_iota(jnp.int32, (2 * blk, 1), 0) < blk
    dim_lo = lax.broadcasted_iota(jnp.int32, (LANES, n_keys), 0) < HEAD_DIM
    lane_lo = lax.broadcasted_iota(jnp.int32, (n_keys, LANES), 1) < HEAD_DIM
    lane_o = lax.broadcasted_iota(jnp.int32, (2 * blk, LANES), 1) < HEAD_DIM
    zero = jnp.zeros((), BF16)
    ones_lo = jnp.where(lane_lo, 1.0, 0.0).astype(BF16)
    ones_hi = jnp.where(lane_lo, 0.0, 1.0).astype(BF16)
    row3 = lax.broadcasted_iota(jnp.int32, (blk, 3 * blk), 0)
    col3 = lax.broadcasted_iota(jnp.int32, (blk, 3 * blk), 1) - blk
    pool_band = [((col3 >= row3 - w // 2) & (col3 <= row3 + w // 2 - 1)).astype(BF16) for w in POOL_WINDOWS]
    row_pos = lax.broadcasted_iota(jnp.int32, (blk, LANES), 0)

    for j in range(nsub):
        n = tstep * nsub + j
        pblk = jnp.maximum(n - 1, 0)
        nblk = jnp.minimum(n + 1, nb - 1)
        r0 = pl.multiple_of(n * blk, blk)
        ps = pl.multiple_of(pblk * blk, blk)
        ns = pl.multiple_of(nblk * blk, blk)
        has_prev = n > 0
        has_next = n < nb - 1
        rows = slice(j * blk, (j + 1) * blk)

        if band:
            vb = jnp.concatenate([v_ref[pl.ds(ps, blk), :], v_ref[pl.ds(r0, blk), :], v_ref[pl.ds(ns, blk), :],
                                  vc_ref[...]], axis=0)
        else:
            vb = vc_ref[...]
        for hk in range(ATTN_KV_HEADS):
            hd = slice(hk * LANES, (hk + 1) * LANES)
            kt = [kc_ref[i, hd, :] for i in range(kc_ref.shape[0])]
            if band:
                kt = [k_ref[pblk, hd, :], k_ref[n, hd, :], k_ref[nblk, hd, :]] + kt
            kt = jnp.concatenate(kt, axis=1)
            vv = vb[:, hd]
            v_aug = jnp.concatenate(
                [jnp.concatenate([jnp.where(lane_lo, vv, zero), ones_lo], axis=1),
                 jnp.concatenate([jnp.where(lane_lo, zero, vv), ones_hi], axis=1)], axis=0)
            c0, c1 = 2 * hk, 2 * hk + 1
            q2 = jnp.concatenate([q_ref[rows, c0 * LANES:(c0 + 1) * LANES],
                                  q_ref[rows, c1 * LANES:(c1 + 1) * LANES]], axis=0)
            probs, esink = [], []
            s_both = _dot(q2, jnp.concatenate([jnp.where(dim_lo, kt, zero), jnp.where(dim_lo, zero, kt)], axis=1))
            for half in range(2):
                s = s_both[:, half * n_keys:(half + 1) * n_keys]
                if band:
                    s = jnp.concatenate([jnp.where(tri_prev & has_prev, s[:, :blk], NEG), s[:, blk:2 * blk],
                                         jnp.where(tri_next & has_next, s[:, 2 * blk:3 * blk], NEG),
                                         s[:, 3 * blk:]], axis=1)
                snk = jnp.where(top_rows, sink_ref[4 * hk + half], sink_ref[4 * hk + 2 + half])
                m = jnp.maximum(jnp.max(s, axis=-1, keepdims=True), snk)
                probs.append(jnp.exp(s - m).astype(BF16))
                esink.append(jnp.exp(snk - m))
            o2 = _dot(jnp.concatenate(probs, axis=1), v_aug)
            o = o2[:, :LANES] / (o2[:, LANES:] + jnp.where(lane_o, esink[0], esink[1]))
            cat_ref[rows, c0 * LANES:(c0 + 1) * LANES] = o[:blk].astype(BF16)
            cat_ref[rows, c1 * LANES:(c1 + 1) * LANES] = o[blk:].astype(BF16)

        uo = u_ref[pl.ds(r0, blk), :]
        up = jnp.where(has_prev, u_ref[pl.ds(ps, blk), :], zero)
        un = jnp.where(has_next, u_ref[pl.ds(ns, blk), :], zero)
        ub = jnp.concatenate([up, uo, un], axis=0)
        pos = r0 + row_pos
        for gi, w in enumerate(POOL_WINDOWS):
            cs = slice(gi * POOL_CH, (gi + 1) * POOL_CH)
            sums = _dot(pool_band[gi], ub[:, cs])
            cnt = jnp.minimum(pos + w // 2, t_len) - jnp.maximum(pos - w // 2, 0)
            dlt = sums / cnt.astype(F32) - uo[:, cs].astype(F32)
            y = _dot(dlt.astype(BF16), pw_ref[gi]) * ps_ref[:, cs]
            off = ATTN_Q_HEADS * HEAD_DIM + gi * POOL_CH
            cat_ref[rows, off:off + POOL_CH] = y.astype(BF16)

    y = _dot(cat_ref[...], wo_ref[...])
    o_ref[...] = x_ref[...] + mod_ref[2:3, :] * _rms(y, g_ref[...])


def _mix0(x, q, k4, v4, kc4, vc4, u, sink, pool_w, pool_scale, w_out, g, mod, mod_row):
    bsz, t, d = x.shape
    band = k4 is not None
    tq = min(t, LIGHT_ROWS)
    n_ctx = vc4.shape[1]
    row = (lambda b, i: (b, 0, 0)) if mod_row is None else (lambda b, i: (mod_row, 0, 0))
    full = lambda n, w: pl.BlockSpec((None, n, w), lambda b, i: (b, 0, 0))
    keys_t = lambda a: pl.BlockSpec((None,) + a.shape[1:], lambda b, i: (b, 0, 0, 0))
    in_specs = [
        pl.BlockSpec(memory_space=pltpu.SMEM),
        pl.BlockSpec((None, tq, d), lambda b, i: (b, i, 0)),
        pl.BlockSpec((None, tq, q.shape[2]), lambda b, i: (b, i, 0)),
    ]
    args = [sink, x, q]
    if band:
        in_specs += [keys_t(k4), full(t, v4.shape[2])]
        args += [k4, v4]
    in_specs += [keys_t(kc4), full(n_ctx, vc4.shape[2]), full(t, u.shape[2]),
                 _const_spec(pool_w.shape), _const_spec(pool_scale.shape), _const_spec(w_out.shape),
                 _const_spec(g.shape), pl.BlockSpec((None, 6, d), row)]
    args += [kc4, vc4, u, pool_w, pool_scale, w_out, g, mod]
    return pl.pallas_call(
        functools.partial(_mix0_kernel, band=band, t_len=t, tq=tq),
        out_shape=jax.ShapeDtypeStruct((bsz, t, d), F32),
        grid=(bsz, t // tq),
        in_specs=in_specs,
        out_specs=pl.BlockSpec((None, tq, d), lambda b, i: (b, i, 0)),
        scratch_shapes=[pltpu.VMEM((tq, w_out.shape[0]), BF16)],
        compiler_params=_params("parallel", "parallel"),
        name="mix0_band" if band else "mix0_ctx",
    )(*args)


HALO = SUBLANES
FFN_CHUNK = 256
FFN_ROWS = 512
LIGHT_ROWS = 1024


def _halo_specs(tm, t, d):
    nblk = tm // HALO
    last = t // HALO - 1
    return [
        pl.BlockSpec((None, tm, d), lambda b, i: (b, i, 0)),
        pl.BlockSpec((None, HALO, d), lambda b, i: (b, jnp.maximum(i * nblk - 1, 0), 0)),
        pl.BlockSpec((None, HALO, d), lambda b, i: (b, jnp.minimum((i + 1) * nblk, last), 0)),
    ]


def _conv3(u, cw, cb, tm):
    return (cw[0:1, :] * u[HALO - 1:HALO - 1 + tm, :] + cw[1:2, :] * u[HALO:HALO + tm, :]
            + cw[2:3, :] * u[HALO + 1:HALO + 1 + tm, :] + cb)


def _ffn_kernel(xm_ref, xp_ref, xn_ref, mod_ref, gpre_ref, gpost_ref, wup_ref, cw_ref, cb_ref, wdn_ref,
                o_ref, hb_ref, acc_ref, u0_ref, u1_ref, *, tm, nt):
    i = pl.program_id(1)
    shift, scale, gate = mod_ref[3:4, :], mod_ref[4:5, :], mod_ref[5:6, :]
    gpre = gpre_ref[...]

    def pre(xv):
        return _rms(xv, gpre) * (1.0 + scale) + shift

    hp = jnp.where(i > 0, pre(xp_ref[...]), 0.0)
    hn = jnp.where(i < nt - 1, pre(xn_ref[...]), 0.0)
    hb_ref[...] = jnp.concatenate([hp, pre(xm_ref[...]), hn], axis=0).astype(BF16)
    acc_ref[...] = jnp.zeros_like(acc_ref)
    dff = wdn_ref.shape[0]
    nchunk = dff // FFN_CHUNK
    half = FFN_CHUNK // LANES
    u_refs = (u0_ref, u1_ref)

    def cols(c, j):
        o = (dff if j >= half else 0) + c * FFN_CHUNK + (j % half) * LANES
        return slice(o, o + LANES)

    def up(c, s):
        for part in range(2):
            o = part * dff + c * FFN_CHUNK
            u = _dot(hb_ref[...], wup_ref[:, o:o + FFN_CHUNK])
            for j in range(half):
                u_refs[s][part * half + j] = u[:, j * LANES:(j + 1) * LANES]

    def conv_act(c, s):
        ur = u_refs[s]

        def conv(j):
            cs = cols(c, j)
            return (cw_ref[0:1, cs] * ur[j, pl.ds(HALO - 1, tm), :] + cw_ref[1:2, cs] * ur[j, pl.ds(HALO, tm), :]
                    + cw_ref[2:3, cs] * ur[j, pl.ds(HALO + 1, tm), :] + cb_ref[:, cs])

        acts = []
        for j in range(half):
            ag, av = conv(j), conv(half + j)
            acts.append((ag * _sigmoid(ag) * av).astype(BF16))
        return jnp.concatenate(acts, axis=1)

    def down(c, act):
        acc_ref[...] += _dot(act, wdn_ref[c * FFN_CHUNK:(c + 1) * FFN_CHUNK, :])

    up(0, 0)
    prev = None
    for c in range(nchunk):
        if c + 1 < nchunk:
            up(c + 1, (c + 1) % 2)
        if prev is not None:
            down(c - 1, prev)
        prev = conv_act(c, c % 2)
    down(nchunk - 1, prev)
    o_ref[...] = xm_ref[...] + gate * _rms(acc_ref[...], gpost_ref[...])


def _ffn(x, mod, mod_row, gpre, gpost, wup, cw, cb, wdn):
    bsz, t, d = x.shape
    tm = min(t, FFN_ROWS)
    nt = t // tm
    assert wdn.shape[0] % FFN_CHUNK == 0
    row = (lambda b, i: (b, 0, 0)) if mod_row is None else (lambda b, i: (mod_row, 0, 0))
    in_specs = _halo_specs(tm, t, d) + [
        pl.BlockSpec((None, 6, d), row), _const_spec(gpre.shape), _const_spec(gpost.shape),
        _const_spec(wup.shape), _const_spec(cw.shape), _const_spec(cb.shape), _const_spec(wdn.shape)]
    u_buf = pltpu.VMEM((2 * FFN_CHUNK // LANES, tm + 2 * HALO, LANES), F32)
    return pl.pallas_call(
        functools.partial(_ffn_kernel, tm=tm, nt=nt),
        out_shape=jax.ShapeDtypeStruct((bsz, t, d), F32),
        grid=(bsz, nt),
        in_specs=in_specs,
        out_specs=pl.BlockSpec((None, tm, d), lambda b, i: (b, i, 0)),
        scratch_shapes=[pltpu.VMEM((tm + 2 * HALO, d), BF16), pltpu.VMEM((tm, d), F32), u_buf, u_buf],
        compiler_params=_params("parallel", "parallel"),
        name="ffn",
    )(x, x, x, mod, gpre, gpost, wup, cw, cb, wdn)


def _in1_kernel(*refs, tm, nt, full):
    if full:
        (xm_ref, xp_ref, xn_ref, mod_ref, g_ref, w_ref, wg_ref, gb_ref, cw_ref, cb_ref, qw_ref, kw_ref,
         uc_ref, og_ref, v_ref, q_ref, k_ref, gt_ref) = refs
    else:
        (xm_ref, xp_ref, xn_ref, mod_ref, g_ref, w_ref, wg_ref, gb_ref, cw_ref, cb_ref, kw_ref,
         v_ref, k_ref, gt_ref) = refs
    i = pl.program_id(1)
    shift, scale = mod_ref[0:1, :], mod_ref[1:2, :]
    g = g_ref[...]

    def pre(xv):
        return _rms(xv, g) * (1.0 + scale) + shift

    hm = pre(xm_ref[...])
    hp = jnp.where(i > 0, pre(xp_ref[...]), 0.0)
    hn = jnp.where(i < nt - 1, pre(xn_ref[...]), 0.0)
    hext = jnp.concatenate([hp, hm, hn], axis=0).astype(BF16)
    dv = v_ref.shape[1]
    u = _dot(hext, w_ref[:, :dv])
    uc = _conv3(u, cw_ref[...], cb_ref[...], tm)
    uc = uc * _sigmoid(uc)
    ucb = uc.astype(BF16)
    hmb = hm.astype(BF16)
    v_ref[...] = _dot(hmb, w_ref[:, dv:2 * dv]).astype(BF16)
    gt_ref[...] = _dot(hmb, wg_ref[...]) + gb_ref[...]
    if full:
        og_ref[...] = _dot(hmb, w_ref[:, 2 * dv:3 * dv]).astype(BF16)
        uc_ref[...] = ucb
    dh = MLSTM_HEAD_DIM
    for hh in range(MLSTM_HEADS):
        uh = ucb[:, hh * dh:(hh + 1) * dh]
        k_ref[:, hh * dh:(hh + 1) * dh] = _dot(uh, kw_ref[hh]).astype(BF16)
        if full:
            q_ref[:, hh * dh:(hh + 1) * dh] = _dot(uh, qw_ref[hh]).astype(BF16)


def _in1(x, mod, mod_row, g, w, wg, gb, cw, cb, qw, kw, full):
    bsz, t, d = x.shape
    tm = min(t, LIGHT_ROWS)
    nt = t // tm
    dw = w.shape[1] // 3
    row = (lambda b, i: (b, 0, 0)) if mod_row is None else (lambda b, i: (mod_row, 0, 0))
    in_specs = _halo_specs(tm, t, d) + [pl.BlockSpec((None, 6, d), row)]
    consts = [g, w, wg, gb, cw, cb] + ([qw] if full else []) + [kw]
    in_specs += [_const_spec(a.shape) for a in consts]
    act = lambda: jax.ShapeDtypeStruct((bsz, t, dw), BF16)
    gates = jax.ShapeDtypeStruct((bsz, t, LANES), F32)
    out_shape = [act(), act(), act(), act(), act(), gates] if full else [act(), act(), gates]
    out_specs = [pl.BlockSpec((None, tm, s.shape[2]), lambda b, i: (b, i, 0)) for s in out_shape]
    return pl.pallas_call(
        functools.partial(_in1_kernel, tm=tm, nt=nt, full=full),
        out_shape=out_shape,
        grid=(bsz, nt),
        in_specs=in_specs,
        out_specs=out_specs,
        compiler_params=_params("parallel", "parallel"),
        name="in1_full" if full else "in1_ctx",
    )(x, x, x, mod, *consts)


def _log_sigmoid(x):
    return jnp.minimum(x, 0.0) - jnp.log1p(jnp.exp(-jnp.abs(x)))


def _scan_lanes(x, op, fill, reverse):
    n = x.shape[1]
    lane = lax.broadcasted_iota(jnp.int32, x.shape, 1)
    d = 1
    while d < n:
        if reverse:
            shifted = jnp.where(lane < n - d, pltpu.roll(x, n - d, 1), fill)
        else:
            shifted = jnp.where(lane >= d, pltpu.roll(x, d, 1), fill)
        x = op(x, shifted)
        d *= 2
    return x


def _mlstm_kernel(*refs, nc, heads):
    io, states = refs[:12], refs[12:]
    for hh in range(heads):
        _mlstm_head(hh, *io, states[2 * hh], states[2 * hh + 1], nc=nc)


def _mlstm_head(hh, q_ref, k_ref, v_ref, kc_ref, vc_ref, gl_ref, gc_ref, og_ref, uc_ref, ng_ref, skip_ref,
                y_ref, cf_ref, cb_ref, *, nc):
    ln = MLSTM_CHUNK
    dh = MLSTM_HEAD_DIM
    hs = slice(hh * dh, (hh + 1) * dh)
    nr = nc + 1
    scores = [_dot_nt(q_ref[c * ln:(c + 1) * ln, hs], k_ref[c * ln:(c + 1) * ln, hs]) for c in range(nc)]
    gates = [jnp.concatenate([gc_ref[hh, gi], gl_ref[hh, gi]], axis=0) for gi in range(4)]

    def direction(ig, fpre, reverse):
        b = _scan_lanes(_log_sigmoid(fpre), jnp.add, 0.0, reverse)
        a = ig - b
        cmax = _scan_lanes(a, jnp.maximum, NEG, reverse)
        e = 0 if reverse else ln - 1
        b_end, a_max = b[:, e:e + 1], cmax[:, e:e + 1]
        order = [0] + (list(range(nc, 0, -1)) if reverse else list(range(1, nr)))
        m_in = [None] * nr
        m = jnp.zeros((1, 1), F32)
        for r in order:
            m_in[r] = m
            m = b_end[r:r + 1] + jnp.maximum(m, a_max[r:r + 1])
        m_in = jnp.concatenate(m_in, axis=0)
        big_m = jnp.maximum(m_in, cmax)
        m_end = jnp.maximum(m_in, a_max)
        return dict(a=a, M=big_m, mt=b + big_m, m_in=m_in, wts=jnp.exp(a - m_end), decay=jnp.exp(m_in - m_end))

    fw = direction(gates[0], gates[1], False)
    bw = direction(gates[2], gates[3], True)

    def lane_bcast(row):
        return jnp.transpose(jnp.broadcast_to(row, (LANES, ln)))

    def wide(a):
        return jnp.concatenate([a] * (dh // LANES), axis=1)

    ones = jnp.ones((ln, LANES), BF16)

    def chunk_kv(r):
        if r == 0:
            return kc_ref[:, hs], vc_ref[:, hs]
        return k_ref[(r - 1) * ln:r * ln, hs], v_ref[(r - 1) * ln:r * ln, hs]

    contrib_f, contrib_b = [], []
    for r in range(nr):
        kk, vv = chunk_kv(r)
        kt = jnp.transpose(kk.astype(F32))
        v_aug = jnp.concatenate([vv, ones], axis=1)
        contrib_f.append(_dot((kt * fw["wts"][r:r + 1]).astype(BF16), v_aug))
        contrib_b.append(_dot((kt * bw["wts"][r:r + 1]).astype(BF16), v_aug))

    st = contrib_f[0]
    for c in range(nc):
        cf_ref[c] = st.astype(BF16)
        if c + 1 < nc:
            st = fw["decay"][c + 1:c + 2] * st + contrib_f[c + 1]
    st = contrib_b[0]
    for c in range(nc - 1, -1, -1):
        cb_ref[c] = st.astype(BF16)
        if c > 0:
            st = bw["decay"][c + 1:c + 2] * st + contrib_b[c + 1]

    ti = lax.broadcasted_iota(jnp.int32, (ln, ln), 0)
    sj = lax.broadcasted_iota(jnp.int32, (ln, ln), 1)
    ng = ng_ref[:, hs]
    skip = skip_ref[:, hs]
    def finish(c, lhs, vv):
        rhs = jnp.concatenate([vv, cf_ref[c, :, :dh], cb_ref[c, :, :dh]], axis=0)
        hsum = _dot(lhs, rhs)
        rows = slice(c * ln, (c + 1) * ln)
        y = _sigmoid(og_ref[rows, hs].astype(F32)) * (_rms(hsum, ng) + skip * uc_ref[rows, hs].astype(F32))
        y_ref[rows, hs] = y.astype(BF16)

    pending = None
    for c in range(nc):
        r = c + 1
        qc = q_ref[c * ln:(c + 1) * ln, hs]
        kk, vv = chunk_kv(r)
        s = scores[c]
        qf = qc.astype(F32)
        lhs = []
        p_sum = None
        for dirn, causal, st_ref in ((fw, sj <= ti, cf_ref), (bw, sj >= ti, cb_ref)):
            big_m = lane_bcast(dirn["M"][r:r + 1])
            p = s * jnp.exp(jnp.where(causal, dirn["a"][r:r + 1] - wide(big_m), NEG))
            iw = jnp.exp(dirn["m_in"][r:r + 1] - big_m)
            den = _dot(p.astype(BF16), ones) + iw * _dot(qc, st_ref[c, :, dh:])
            rinv = 1.0 / jnp.maximum(jnp.abs(den), jnp.exp(-lane_bcast(dirn["mt"][r:r + 1])))
            p_sum = p * wide(rinv) if p_sum is None else p_sum + p * wide(rinv)
            lhs.append((qf * wide(iw * rinv)).astype(BF16))
        if pending is not None:
            finish(*pending)
        pending = (c, jnp.concatenate([p_sum.astype(BF16)] + lhs, axis=1), vv)
    finish(*pending)


def _mlstm(q, k, v, kc, vc, gl, gc, og, uc, ng, skip):
    bsz, t, wd = q.shape
    dh = MLSTM_HEAD_DIM
    nh = wd // dh
    nc = t // MLSTM_CHUNK
    n_ctx = kc.shape[1]
    assert n_ctx == MLSTM_CHUNK and gl.shape == (bsz, nh, 4, nc, MLSTM_CHUNK)
    hg = MLSTM_HEADS_PER_STEP
    assert nh % hg == 0
    seq = lambda n: pl.BlockSpec((None, n, hg * dh), lambda b, h: (b, 0, h))
    head_vec = pl.BlockSpec((1, hg * dh), lambda b, h: (0, h))
    state = pltpu.VMEM((nc, dh, dh + LANES), BF16)
    return pl.pallas_call(
        functools.partial(_mlstm_kernel, nc=nc, heads=hg),
        out_shape=jax.ShapeDtypeStruct((bsz, t, wd), BF16),
        grid=(bsz, nh // hg),
        in_specs=[seq(t), seq(t), seq(t), seq(n_ctx), seq(n_ctx),
                  pl.BlockSpec((None, hg, 4, nc, MLSTM_CHUNK), lambda b, h: (b, h, 0, 0, 0)),
                  pl.BlockSpec((None, hg, 4, 1, MLSTM_CHUNK), lambda b, h: (b, h, 0, 0, 0)),
                  seq(t), seq(t), head_vec, head_vec],
        out_specs=seq(t),
        scratch_shapes=[state] * (2 * hg),
        compiler_params=_params("parallel", "parallel"),
        name="mlstm",
    )(q, k, v, kc, vc, gl, gc, og, uc, ng, skip)


def _out1_kernel(x_ref, y_ref, wo_ref, g_ref, mod_ref, o_ref):
    y = _dot(y_ref[...], wo_ref[...])
    o_ref[...] = x_ref[...] + mod_ref[2:3, :] * _rms(y, g_ref[...])


def _out1(x, y, w_out, g, mod):
    bsz, t, d = x.shape
    tm = min(t, LIGHT_ROWS)
    wd = y.shape[2]
    tile = lambda n: pl.BlockSpec((None, tm, n), lambda b, i: (b, i, 0))
    return pl.pallas_call(
        _out1_kernel,
        out_shape=jax.ShapeDtypeStruct((bsz, t, d), F32),
        grid=(bsz, t // tm),
        in_specs=[tile(d), tile(wd), _const_spec(w_out.shape), _const_spec(g.shape),
                  pl.BlockSpec((None, 6, d), lambda b, i: (b, 0, 0))],
        out_specs=tile(d),
        compiler_params=_params("parallel", "parallel"),
        name="out1",
    )(x, y, w_out, g, mod)


def _rope_tables(t_len, scale):
    rows = t_len // GRID_W
    row = jnp.repeat(jnp.arange(rows, dtype=F32), GRID_W)
    col = jnp.tile(jnp.arange(GRID_W, dtype=F32), rows)
    n_freq = HEAD_DIM // 4
    inv = ROPE_BASE ** (-jnp.arange(n_freq, dtype=F32) / n_freq)
    ar, ac = row[:, None] * inv, col[:, None] * inv
    cos = jnp.concatenate([jnp.cos(ar), jnp.cos(ar), jnp.cos(ac), jnp.cos(ac)], axis=-1)
    sin = jnp.concatenate([-jnp.sin(ar), jnp.sin(ar), -jnp.sin(ac), jnp.sin(ac)], axis=-1)
    reps = LANES // HEAD_DIM
    return jnp.tile(cos, (1, reps)) * scale, jnp.tile(sin, (1, reps)) * scale


def _ffn_weights(w_up, conv_w, conv_b, w_down):
    return w_up.astype(BF16), conv_w, conv_b.reshape(1, -1), w_down.astype(BF16)


def kernel(x, c, ctx, c_ctx, mod_w, mod_b, norm_g, attn_in_w, attn_sink, pool_w, pool_scale, attn_out_w,
           rec_in_w, rec_gate_b, rec_conv_w, rec_conv_b, rec_q_w, rec_k_w, rec_norm_g, rec_skip, rec_out_w,
           ffn_up_w, ffn_conv_w, ffn_conv_b, ffn_down_w):
    bsz, t, d = x.shape
    n_ctx = ctx.shape[1]
    ctx_row = bsz

    pad_rows = (-(bsz + 1)) % 16
    c_all = jnp.concatenate([c, c_ctx[None, :], jnp.zeros((pad_rows, d), F32)], axis=0)
    mod = _adaln(c_all, mod_w, mod_b)
    mod = mod.reshape(mod.shape[0], mod.shape[1], 6, d)
    row2 = lambda a: a.reshape(1, -1)

    w_in0 = attn_in_w[0].astype(BF16)
    qscale = HEAD_DIM ** -0.5
    tables = _rope_tables(t, qscale) + _rope_tables(t, 1.0)
    g00 = row2(norm_g[0, 0])
    q, k4, v4, u = _in0(x, mod[0], None, g00, w_in0, tables)
    qc, kc4, vc4, ucx = _in0(ctx, mod[0], ctx_row, g00, w_in0, None)
    pw = pool_w[0].astype(BF16)
    psc = row2(pool_scale[0])
    wo0 = attn_out_w[0].astype(BF16)
    g01 = row2(norm_g[0, 1])
    sink = attn_sink[0]
    x1 = _mix0(x, q, k4, v4, kc4, vc4, u, sink, pw, psc, wo0, g01, mod[0], None)
    c1 = _mix0(ctx, qc, None, None, kc4, vc4, ucx, sink, pw, psc, wo0, g01, mod[0], ctx_row)
    f0 = _ffn_weights(ffn_up_w[0], ffn_conv_w[0], ffn_conv_b[0], ffn_down_w[0])
    g02, g03 = row2(norm_g[0, 2]), row2(norm_g[0, 3])
    x2 = _ffn(x1, mod[0], None, g02, g03, *f0)
    c2 = _ffn(c1, mod[0], ctx_row, g02, g03, *f0)

    w = rec_in_w[0]
    wd = MLSTM_HEADS * MLSTM_HEAD_DIM
    ng = 4 * MLSTM_HEADS
    w_main = w[:, :3 * wd].astype(BF16)
    w_gate = jnp.concatenate([w[:, 3 * wd:], jnp.zeros((d, LANES - ng), F32)], axis=1).astype(BF16)
    gbias = jnp.concatenate([rec_gate_b[0].reshape(1, ng), jnp.zeros((1, LANES - ng), F32)], axis=1)
    cw, cb = rec_conv_w[0], row2(rec_conv_b[0])
    qw = rec_q_w[0].astype(BF16)
    kw = (rec_k_w[0] * (MLSTM_HEAD_DIM ** -0.5)).astype(BF16)
    g10 = row2(norm_g[1, 0])
    uc, og, v, qm, km, gts = _in1(x2, mod[1], None, g10, w_main, w_gate, gbias, cw, cb, qw, kw, True)
    vcx, kcx, gtc = _in1(c2, mod[1], ctx_row, g10, w_main, w_gate, gbias, cw, cb, None, kw, False)

    def gate_rows(gt, n):
        gt = gt[:, :, :ng].reshape(bsz, n // MLSTM_CHUNK, MLSTM_CHUNK, 4, MLSTM_HEADS)
        return gt.transpose(0, 4, 3, 1, 2)

    y = _mlstm(qm, km, v, kcx, vcx, gate_rows(gts, t), gate_rows(gtc, n_ctx), og, uc,
               row2(rec_norm_g[0]), row2(rec_skip[0]))
    x3 = _out1(x2, y, rec_out_w[0].astype(BF16), row2(norm_g[1, 1]), mod[1])
    f1 = _ffn_weights(ffn_up_w[1], ffn_conv_w[1], ffn_conv_b[1], ffn_down_w[1])
    return _ffn(x3, mod[1], None, row2(norm_g[1, 2]), row2(norm_g[1, 3]), *f1)
```

```python
import functools

import jax
import jax.numpy as jnp
from jax import lax
from jax.experimental import pallas as pl
from jax.experimental.pallas import tpu as pltpu

F32 = jnp.float32
BF16 = jnp.bfloat16

GRID_W = 64
HEAD_DIM = 64
ATTN_Q_HEADS = 8
ATTN_KV_HEADS = 2
ATTN_BLOCK = 128
ROPE_BASE = 10000.0
POOL_WINDOWS = (2, 4, 8, 16)
POOL_CH = 128
MLSTM_HEADS = 4
MLSTM_HEAD_DIM = 256
MLSTM_CHUNK = 256
MLSTM_HEADS_PER_STEP = 2
EPS = 1e-6
NEG = -1e30

LANES = 128
SUBLANES = 8
VMEM_LIMIT = 56 * 1024 * 1024


def _params(*sem):
    return pltpu.CompilerParams(dimension_semantics=sem, vmem_limit_bytes=VMEM_LIMIT)


def _rms(x, g):
    return x * lax.rsqrt(jnp.mean(x * x, axis=-1, keepdims=True) + EPS) * g


def _sigmoid(x):
    return 1.0 / (1.0 + jnp.exp(-x))


def _dot(a, b):
    return jnp.dot(a, b, preferred_element_type=F32)


def _dot_nt(a, b):
    return lax.dot_general(a, b, (((1,), (1,)), ((), ())), preferred_element_type=F32)


def _dot_tn(a, b):
    return lax.dot_general(a, b, (((0,), (0,)), ((), ())), preferred_element_type=F32)


def _const_spec(shape):
    nd = len(shape)
    return pl.BlockSpec(shape, lambda *_: (0,) * nd)


def _adaln_kernel(c_ref, w_ref, b_ref, o_ref):
    c = c_ref[...]
    s = c * _sigmoid(c)
    o_ref[...] = _dot(s.astype(BF16), w_ref[...].astype(BF16)) + b_ref[...]


def _adaln(c_all, mod_w, mod_b):
    depth, d, n6 = mod_w.shape
    rows = c_all.shape[0]
    tn = 1536
    return pl.pallas_call(
        _adaln_kernel,
        out_shape=jax.ShapeDtypeStruct((depth, rows, n6), F32),
        grid=(depth, n6 // tn),
        in_specs=[
            pl.BlockSpec((rows, d), lambda l, j: (0, 0)),
            pl.BlockSpec((None, d, tn), lambda l, j: (l, 0, j)),
            pl.BlockSpec((None, 1, tn), lambda l, j: (l, 0, j)),
        ],
        out_specs=pl.BlockSpec((None, rows, tn), lambda l, j: (l, 0, j)),
        compiler_params=_params("arbitrary", "arbitrary"),
        name="adaln",
    )(c_all, mod_w, mod_b.reshape(depth, 1, n6))


def _in0_kernel(*refs, rope):
    if rope:
        x_ref, mod_ref, g_ref, w_ref, cq_ref, sq_ref, ck_ref, sk_ref, q_ref, k_ref, v_ref, u_ref = refs
    else:
        x_ref, mod_ref, g_ref, w_ref, q_ref, k_ref, v_ref, u_ref = refs
    x = x_ref[...]
    h = _rms(x, g_ref[...]) * (1.0 + mod_ref[1:2, :]) + mod_ref[0:1, :]
    y = _dot(h.astype(BF16), w_ref[...])
    nq = q_ref.shape[1]
    nkv = ATTN_KV_HEADS * HEAD_DIM
    tm = x.shape[0]
    lane = lax.broadcasted_iota(jnp.int32, (tm, LANES), 1)
    first = lane < HEAD_DIM

    def dup(a):
        sw = pltpu.roll(a, HEAD_DIM, 1)
        return jnp.concatenate([jnp.where(first, a, sw), jnp.where(first, sw, a)], axis=1)

    def put_keys(kf):
        for jb in range(tm // ATTN_BLOCK):
            k_ref[jb] = jnp.transpose(kf[jb * ATTN_BLOCK:(jb + 1) * ATTN_BLOCK, :]).astype(BF16)

    if rope:
        even = (lane // 16) % 2 == 0

        def rot(a, c, s):
            sw = jnp.where(even, pltpu.roll(a, LANES - 16, 1), pltpu.roll(a, 16, 1))
            return a * c + sw * s

        cq, sq, ck, sk = cq_ref[...], sq_ref[...], ck_ref[...], sk_ref[...]
        for j in range(nq // LANES):
            q_ref[:, j * LANES:(j + 1) * LANES] = rot(y[:, j * LANES:(j + 1) * LANES], cq, sq).astype(BF16)
        put_keys(dup(rot(y[:, nq:nq + nkv], ck, sk)))
    else:
        q_ref[...] = (y[:, :nq] * (HEAD_DIM ** -0.5)).astype(BF16)
        put_keys(dup(y[:, nq:nq + nkv]))
    v_ref[...] = dup(y[:, nq + nkv:nq + 2 * nkv]).astype(BF16)
    u_ref[...] = y[:, nq + 2 * nkv:].astype(BF16)


def _in0(x, mod, mod_row, g, w, tables):
    bsz, t, d = x.shape
    tm = min(t, STREAM_ROWS)
    rope = tables is not None
    nq, nk, nu = 512, 256, 512
    row = (lambda b, i: (b, 0, 0)) if mod_row is None else (lambda b, i: (mod_row, 0, 0))
    in_specs = [
        pl.BlockSpec((None, tm, d), lambda b, i: (b, i, 0)),
        pl.BlockSpec((None, 6, d), row),
        _const_spec((1, d)),
        _const_spec(w.shape),
    ]
    args = [x, mod, g, w]
    if rope:
        in_specs += [pl.BlockSpec((tm, LANES), lambda b, i: (i, 0))] * 4
        args += list(tables)
    rows = lambda n: (jax.ShapeDtypeStruct((bsz, t, n), BF16), pl.BlockSpec((None, tm, n), lambda b, i: (b, i, 0)))
    keys_t = (jax.ShapeDtypeStruct((bsz, t // ATTN_BLOCK, nk, ATTN_BLOCK), BF16),
              pl.BlockSpec((None, tm // ATTN_BLOCK, nk, ATTN_BLOCK), lambda b, i: (b, i, 0, 0)))
    outs = [rows(nq), keys_t, rows(nk), rows(nu)]
    return pl.pallas_call(
        functools.partial(_in0_kernel, rope=rope),
        out_shape=[o[0] for o in outs],
        grid=(bsz, t // tm),
        in_specs=in_specs,
        out_specs=[o[1] for o in outs],
        compiler_params=_params("parallel", "parallel"),
        name="in0_rope" if rope else "in0_ctx",
    )(*args)


def _mix0_kernel(*refs, band, t_len, tq):
    if band:
        (sink_ref, x_ref, q_ref, k_ref, v_ref, kc_ref, vc_ref, u_ref, pw_ref, ps_ref, wo_ref, g_ref, mod_ref,
         o_ref, cat_ref) = refs
    else:
        (sink_ref, x_ref, q_ref, kc_ref, vc_ref, u_ref, pw_ref, ps_ref, wo_ref, g_ref, mod_ref,
         o_ref, cat_ref) = refs
    blk = ATTN_BLOCK
    nb = t_len // blk
    nsub = tq // blk
    tstep = pl.program_id(1)
    n_ctx = vc_ref.shape[0]
    n_keys = (3 * blk if band else 0) + n_ctx

    qrow = jnp.bitwise_and(lax.broadcasted_iota(jnp.int32, (2 * blk, blk), 0), blk - 1)
    kcol = lax.broadcasted_iota(jnp.int32, (2 * blk, blk), 1)
    tri_prev = kcol >= qrow
    tri_next = kcol <= qrow
    top_rows = lax.broadcasted_iota(jnp.int32, (2 * blk, 1), 0) < blk
    dim_lo = lax.broadcasted_iota(jnp.int32, (LANES, n_keys), 0) < HEAD_DIM
    lane_lo = lax.broadcasted_iota(jnp.int32, (n_keys, LANES), 1) < HEAD_DIM
    lane_o = lax.broadcasted_iota(jnp.int32, (2 * blk, LANES), 1) < HEAD_DIM
    zero = jnp.zeros((), BF16)
    ones_lo = jnp.where(lane_lo, 1.0, 0.0).astype(BF16)
    ones_hi = jnp.where(lane_lo, 0.0, 1.0).astype(BF16)
    row3 = lax.broadcasted_iota(jnp.int32, (blk, 3 * blk), 0)
    col3 = lax.broadcasted_iota(jnp.int32, (blk, 3 * blk), 1) - blk
    pool_band = [((col3 >= row3 - w // 2) & (col3 <= row3 + w // 2 - 1)).astype(BF16) for w in POOL_WINDOWS]
    row_pos = lax.broadcasted_iota(jnp.int32, (blk, LANES), 0)

    for j in range(nsub):
        n = tstep * nsub + j
        pblk = jnp.maximum(n - 1, 0)
        nblk = jnp.minimum(n + 1, nb - 1)
        r0 = pl.multiple_of(n * blk, blk)
        ps = pl.multiple_of(pblk * blk, blk)
        ns = pl.multiple_of(nblk * blk, blk)
        has_prev = n > 0
        has_next = n < nb - 1
        rows = slice(j * blk, (j + 1) * blk)

        if band:
            vb = jnp.concatenate([v_ref[pl.ds(ps, blk), :], v_ref[pl.ds(r0, blk), :], v_ref[pl.ds(ns, blk), :],
                                  vc_ref[...]], axis=0)
        else:
            vb = vc_ref[...]
        for hk in range(ATTN_KV_HEADS):
            hd = slice(hk * LANES, (hk + 1) * LANES)
            kt = [kc_ref[i, hd, :] for i in range(kc_ref.shape[0])]
            if band:
                kt = [k_ref[pblk, hd, :], k_ref[n, hd, :], k_ref[nblk, hd, :]] + kt
            kt = jnp.concatenate(kt, axis=1)
            vv = vb[:, hd]
            v_aug = jnp.concatenate(
                [jnp.concatenate([jnp.where(lane_lo, vv, zero), ones_lo], axis=1),
                 jnp.concatenate([jnp.where(lane_lo, zero, vv), ones_hi], axis=1)], axis=0)
            c0, c1 = 2 * hk, 2 * hk + 1
            q2 = jnp.concatenate([q_ref[rows, c0 * LANES:(c0 + 1) * LANES],
                                  q_ref[rows, c1 * LANES:(c1 + 1) * LANES]], axis=0)
            probs, esink = [], []
            s_both = _dot(q2, jnp.concatenate([jnp.where(dim_lo, kt, zero), jnp.where(dim_lo, zero, kt)], axis=1))
            for half in range(2):
                s = s_both[:, half * n_keys:(half + 1) * n_keys]
                if band:
                    s = jnp.concatenate([jnp.where(tri_prev & has_prev, s[:, :blk], NEG), s[:, blk:2 * blk],
                                         jnp.where(tri_next & has_next, s[:, 2 * blk:3 * blk], NEG),
                                         s[:, 3 * blk:]], axis=1)
                snk = jnp.where(top_rows, sink_ref[4 * hk + half], sink_ref[4 * hk + 2 + half])
                m = jnp.maximum(jnp.max(s, axis=-1, keepdims=True), snk)
                probs.append(jnp.exp(s - m).astype(BF16))
                esink.append(jnp.exp(snk - m))
            o2 = _dot(jnp.concatenate(probs, axis=1), v_aug)
            o = o2[:, :LANES] / (o2[:, LANES:] + jnp.where(lane_o, esink[0], esink[1]))
            cat_ref[rows, c0 * LANES:(c0 + 1) * LANES] = o[:blk].astype(BF16)
            cat_ref[rows, c1 * LANES:(c1 + 1) * LANES] = o[blk:].astype(BF16)

        uo = u_ref[pl.ds(r0, blk), :]
        up = jnp.where(has_prev, u_ref[pl.ds(ps, blk), :], zero)
        un = jnp.where(has_next, u_ref[pl.ds(ns, blk), :], zero)
        ub = jnp.concatenate([up, uo, un], axis=0)
        pos = r0 + row_pos
        for gi, w in enumerate(POOL_WINDOWS):
            cs = slice(gi * POOL_CH, (gi + 1) * POOL_CH)
            sums = _dot(pool_band[gi], ub[:, cs])
            cnt = jnp.minimum(pos + w // 2, t_len) - jnp.maximum(pos - w // 2, 0)
            dlt = sums / cnt.astype(F32) - uo[:, cs].astype(F32)
            y = _dot(dlt.astype(BF16), pw_ref[gi]) * ps_ref[:, cs]
            off = ATTN_Q_HEADS * HEAD_DIM + gi * POOL_CH
            cat_ref[rows, off:off + POOL_CH] = y.astype(BF16)

    y = _dot(cat_ref[...], wo_ref[...])
    o_ref[...] = x_ref[...] + mod_ref[2:3, :] * _rms(y, g_ref[...])


def _mix0(x, q, k4, v4, kc4, vc4, u, sink, pool_w, pool_scale, w_out, g, mod, mod_row):
    bsz, t, d = x.shape
    band = k4 is not None
    tq = min(t, LIGHT_ROWS)
    n_ctx = vc4.shape[1]
    row = (lambda b, i: (b, 0, 0)) if mod_row is None else (lambda b, i: (mod_row, 0, 0))
    full = lambda n, w: pl.BlockSpec((None, n, w), lambda b, i: (b, 0, 0))
    keys_t = lambda a: pl.BlockSpec((None,) + a.shape[1:], lambda b, i: (b, 0, 0, 0))
    in_specs = [
        pl.BlockSpec(memory_space=pltpu.SMEM),
        pl.BlockSpec((None, tq, d), lambda b, i: (b, i, 0)),
        pl.BlockSpec((None, tq, q.shape[2]), lambda b, i: (b, i, 0)),
    ]
    args = [sink, x, q]
    if band:
        in_specs += [keys_t(k4), full(t, v4.shape[2])]
        args += [k4, v4]
    in_specs += [keys_t(kc4), full(n_ctx, vc4.shape[2]), full(t, u.shape[2]),
                 _const_spec(pool_w.shape), _const_spec(pool_scale.shape), _const_spec(w_out.shape),
                 _const_spec(g.shape), pl.BlockSpec((None, 6, d), row)]
    args += [kc4, vc4, u, pool_w, pool_scale, w_out, g, mod]
    return pl.pallas_call(
        functools.partial(_mix0_kernel, band=band, t_len=t, tq=tq),
        out_shape=jax.ShapeDtypeStruct((bsz, t, d), F32),
        grid=(bsz, t // tq),
        in_specs=in_specs,
        out_specs=pl.BlockSpec((None, tq, d), lambda b, i: (b, i, 0)),
        scratch_shapes=[pltpu.VMEM((tq, w_out.shape[0]), BF16)],
        compiler_params=_params("parallel", "parallel"),
        name="mix0_band" if band else "mix0_ctx",
    )(*args)


HALO = SUBLANES
FFN_CHUNK = 256
FFN_ROWS = 512
LIGHT_ROWS = 1024
STREAM_ROWS = 2048


def _halo_specs(tm, t, d):
    nblk = tm // HALO
    last = t // HALO - 1
    return [
        pl.BlockSpec((None, tm, d), lambda b, i: (b, i, 0)),
        pl.BlockSpec((None, HALO, d), lambda b, i: (b, jnp.maximum(i * nblk - 1, 0), 0)),
        pl.BlockSpec((None, HALO, d), lambda b, i: (b, jnp.minimum((i + 1) * nblk, last), 0)),
    ]


def _conv3(u, cw, cb, tm):
    return (cw[0:1, :] * u[HALO - 1:HALO - 1 + tm, :] + cw[1:2, :] * u[HALO:HALO + tm, :]
            + cw[2:3, :] * u[HALO + 1:HALO + 1 + tm, :] + cb)


def _ffn_kernel(xm_ref, xp_ref, xn_ref, mod_ref, gpre_ref, gpost_ref, wup_ref, cw_ref, cb_ref, wdn_ref,
                o_ref, hb_ref, acc_ref, u0_ref, u1_ref, *, tm, nt):
    i = pl.program_id(1)
    shift, scale, gate = mod_ref[3:4, :], mod_ref[4:5, :], mod_ref[5:6, :]
    gpre = gpre_ref[...]

    def pre(xv):
        return _rms(xv, gpre) * (1.0 + scale) + shift

    hp = jnp.where(i > 0, pre(xp_ref[...]), 0.0)
    hn = jnp.where(i < nt - 1, pre(xn_ref[...]), 0.0)
    hb_ref[...] = jnp.concatenate([hp, pre(xm_ref[...]), hn], axis=0).astype(BF16)
    acc_ref[...] = jnp.zeros_like(acc_ref)
    dff = wdn_ref.shape[0]
    nchunk = dff // FFN_CHUNK
    half = FFN_CHUNK // LANES
    u_refs = (u0_ref, u1_ref)

    def cols(c, j):
        o = (dff if j >= half else 0) + c * FFN_CHUNK + (j % half) * LANES
        return slice(o, o + LANES)

    def up(c, s):
        for part in range(2):
            o = part * dff + c * FFN_CHUNK
            u = _dot(hb_ref[...], wup_ref[:, o:o + FFN_CHUNK])
            for j in range(half):
                u_refs[s][part * half + j] = u[:, j * LANES:(j + 1) * LANES]

    def conv_act(c, s):
        ur = u_refs[s]

        def conv(j):
            cs = cols(c, j)
            return (cw_ref[0:1, cs] * ur[j, pl.ds(HALO - 1, tm), :] + cw_ref[1:2, cs] * ur[j, pl.ds(HALO, tm), :]
                    + cw_ref[2:3, cs] * ur[j, pl.ds(HALO + 1, tm), :] + cb_ref[:, cs])

        acts = []
        for j in range(half):
            ag, av = conv(j), conv(half + j)
            acts.append((ag * _sigmoid(ag) * av).astype(BF16))
        return jnp.concatenate(acts, axis=1)

    def down(c, act):
        acc_ref[...] += _dot(act, wdn_ref[c * FFN_CHUNK:(c + 1) * FFN_CHUNK, :])

    up(0, 0)
    prev = None
    for c in range(nchunk):
        if c + 1 < nchunk:
            up(c + 1, (c + 1) % 2)
        if prev is not None:
            down(c - 1, prev)
        prev = conv_act(c, c % 2)
    down(nchunk - 1, prev)
    o_ref[...] = xm_ref[...] + gate * _rms(acc_ref[...], gpost_ref[...])


def _ffn(x, mod, mod_row, gpre, gpost, wup, cw, cb, wdn):
    bsz, t, d = x.shape
    tm = min(t, FFN_ROWS)
    nt = t // tm
    assert wdn.shape[0] % FFN_CHUNK == 0
    row = (lambda b, i: (b, 0, 0)) if mod_row is None else (lambda b, i: (mod_row, 0, 0))
    in_specs = _halo_specs(tm, t, d) + [
        pl.BlockSpec((None, 6, d), row), _const_spec(gpre.shape), _const_spec(gpost.shape),
        _const_spec(wup.shape), _const_spec(cw.shape), _const_spec(cb.shape), _const_spec(wdn.shape)]
    u_buf = pltpu.VMEM((2 * FFN_CHUNK // LANES, tm + 2 * HALO, LANES), F32)
    return pl.pallas_call(
        functools.partial(_ffn_kernel, tm=tm, nt=nt),
        out_shape=jax.ShapeDtypeStruct((bsz, t, d), F32),
        grid=(bsz, nt),
        in_specs=in_specs,
        out_specs=pl.BlockSpec((None, tm, d), lambda b, i: (b, i, 0)),
        scratch_shapes=[pltpu.VMEM((tm + 2 * HALO, d), BF16), pltpu.VMEM((tm, d), F32), u_buf, u_buf],
        compiler_params=_params("parallel", "parallel"),
        name="ffn",
    )(x, x, x, mod, gpre, gpost, wup, cw, cb, wdn)


def _in1_kernel(*refs, tm, nt, full):
    if full:
        (xm_ref, xp_ref, xn_ref, mod_ref, g_ref, w_ref, wg_ref, gb_ref, cw_ref, cb_ref, qw_ref, kw_ref,
         uc_ref, og_ref, v_ref, q_ref, k_ref, gt_ref) = refs
    else:
        (xm_ref, xp_ref, xn_ref, mod_ref, g_ref, w_ref, wg_ref, gb_ref, cw_ref, cb_ref, kw_ref,
         v_ref, k_ref, gt_ref) = refs
    i = pl.program_id(1)
    shift, scale = mod_ref[0:1, :], mod_ref[1:2, :]
    g = g_ref[...]

    def pre(xv):
        return _rms(xv, g) * (1.0 + scale) + shift

    hm = pre(xm_ref[...])
    hp = jnp.where(i > 0, pre(xp_ref[...]), 0.0)
    hn = jnp.where(i < nt - 1, pre(xn_ref[...]), 0.0)
    hext = jnp.concatenate([hp, hm, hn], axis=0).astype(BF16)
    dv = v_ref.shape[1]
    u = _dot(hext, w_ref[:, :dv])
    uc = _conv3(u, cw_ref[...], cb_ref[...], tm)
    uc = uc * _sigmoid(uc)
    ucb = uc.astype(BF16)
    hmb = hm.astype(BF16)
    v_ref[...] = _dot(hmb, w_ref[:, dv:2 * dv]).astype(BF16)
    gt_ref[...] = _dot(hmb, wg_ref[...]) + gb_ref[...]
    if full:
        og_ref[...] = _dot(hmb, w_ref[:, 2 * dv:3 * dv]).astype(BF16)
        uc_ref[...] = ucb
    dh = MLSTM_HEAD_DIM
    for hh in range(MLSTM_HEADS):
        uh = ucb[:, hh * dh:(hh + 1) * dh]
        k_ref[:, hh * dh:(hh + 1) * dh] = _dot(uh, kw_ref[hh]).astype(BF16)
        if full:
            q_ref[:, hh * dh:(hh + 1) * dh] = _dot(uh, qw_ref[hh]).astype(BF16)


def _in1(x, mod, mod_row, g, w, wg, gb, cw, cb, qw, kw, full):
    bsz, t, d = x.shape
    tm = min(t, LIGHT_ROWS)
    nt = t // tm
    dw = w.shape[1] // 3
    row = (lambda b, i: (b, 0, 0)) if mod_row is None else (lambda b, i: (mod_row, 0, 0))
    in_specs = _halo_specs(tm, t, d) + [pl.BlockSpec((None, 6, d), row)]
    consts = [g, w, wg, gb, cw, cb] + ([qw] if full else []) + [kw]
    in_specs += [_const_spec(a.shape) for a in consts]
    act = lambda: jax.ShapeDtypeStruct((bsz, t, dw), BF16)
    gates = jax.ShapeDtypeStruct((bsz, t, LANES), F32)
    out_shape = [act(), act(), act(), act(), act(), gates] if full else [act(), act(), gates]
    out_specs = [pl.BlockSpec((None, tm, s.shape[2]), lambda b, i: (b, i, 0)) for s in out_shape]
    return pl.pallas_call(
        functools.partial(_in1_kernel, tm=tm, nt=nt, full=full),
        out_shape=out_shape,
        grid=(bsz, nt),
        in_specs=in_specs,
        out_specs=out_specs,
        compiler_params=_params("parallel", "parallel"),
        name="in1_full" if full else "in1_ctx",
    )(x, x, x, mod, *consts)


def _log_sigmoid(x):
    return jnp.minimum(x, 0.0) - jnp.log1p(jnp.exp(-jnp.abs(x)))


def _scan_lanes(x, op, fill, reverse):
    n = x.shape[1]
    lane = lax.broadcasted_iota(jnp.int32, x.shape, 1)
    d = 1
    while d < n:
        if reverse:
            shifted = jnp.where(lane < n - d, pltpu.roll(x, n - d, 1), fill)
        else:
            shifted = jnp.where(lane >= d, pltpu.roll(x, d, 1), fill)
        x = op(x, shifted)
        d *= 2
    return x


def _mlstm_kernel(*refs, nc, heads):
    io, states = refs[:12], refs[12:]
    for hh in range(heads):
        _mlstm_head(hh, *io, states[2 * hh], states[2 * hh + 1], nc=nc)


def _mlstm_head(hh, q_ref, k_ref, v_ref, kc_ref, vc_ref, gl_ref, gc_ref, og_ref, uc_ref, ng_ref, skip_ref,
                y_ref, cf_ref, cb_ref, *, nc):
    ln = MLSTM_CHUNK
    dh = MLSTM_HEAD_DIM
    hs = slice(hh * dh, (hh + 1) * dh)
    nr = nc + 1
    scores = [_dot_nt(q_ref[c * ln:(c + 1) * ln, hs], k_ref[c * ln:(c + 1) * ln, hs]) for c in range(nc)]
    gates = [jnp.concatenate([gc_ref[hh, gi], gl_ref[hh, gi]], axis=0) for gi in range(4)]

    def direction(ig, fpre, reverse):
        b = _scan_lanes(_log_sigmoid(fpre), jnp.add, 0.0, reverse)
        a = ig - b
        cmax = _scan_lanes(a, jnp.maximum, NEG, reverse)
        e = 0 if reverse else ln - 1
        b_end, a_max = b[:, e:e + 1], cmax[:, e:e + 1]
        order = [0] + (list(range(nc, 0, -1)) if reverse else list(range(1, nr)))
        m_in = [None] * nr
        m = jnp.zeros((1, 1), F32)
        for r in order:
            m_in[r] = m
            m = b_end[r:r + 1] + jnp.maximum(m, a_max[r:r + 1])
        m_in = jnp.concatenate(m_in, axis=0)
        big_m = jnp.maximum(m_in, cmax)
        m_end = jnp.maximum(m_in, a_max)
        return dict(a=a, M=big_m, mt=b + big_m, m_in=m_in, wts=jnp.exp(a - m_end), decay=jnp.exp(m_in - m_end))

    fw = direction(gates[0], gates[1], False)
    bw = direction(gates[2], gates[3], True)

    def lane_bcast(row):
        return jnp.transpose(jnp.broadcast_to(row, (LANES, ln)))

    def wide(a):
        return jnp.concatenate([a] * (dh // LANES), axis=1)

    ones = jnp.ones((ln, LANES), BF16)

    def chunk_kv(r):
        if r == 0:
            return kc_ref[:, hs], vc_ref[:, hs]
        return k_ref[(r - 1) * ln:r * ln, hs], v_ref[(r - 1) * ln:r * ln, hs]

    contrib_f, contrib_b = [], []
    for r in range(nr):
        kk, vv = chunk_kv(r)
        kt = jnp.transpose(kk.astype(F32))
        v_aug = jnp.concatenate([vv, ones], axis=1)
        contrib_f.append(_dot((kt * fw["wts"][r:r + 1]).astype(BF16), v_aug))
        contrib_b.append(_dot((kt * bw["wts"][r:r + 1]).astype(BF16), v_aug))

    st = contrib_f[0]
    for c in range(nc):
        cf_ref[c] = st.astype(BF16)
        if c + 1 < nc:
            st = fw["decay"][c + 1:c + 2] * st + contrib_f[c + 1]
    st = contrib_b[0]
    for c in range(nc - 1, -1, -1):
        cb_ref[c] = st.astype(BF16)
        if c > 0:
            st = bw["decay"][c + 1:c + 2] * st + contrib_b[c + 1]

    ti = lax.broadcasted_iota(jnp.int32, (ln, ln), 0)
    sj = lax.broadcasted_iota(jnp.int32, (ln, ln), 1)
    ng = ng_ref[:, hs]
    skip = skip_ref[:, hs]
    def finish(c, lhs, vv):
        rhs = jnp.concatenate([vv, cf_ref[c, :, :dh], cb_ref[c, :, :dh]], axis=0)
        hsum = _dot(lhs, rhs)
        rows = slice(c * ln, (c + 1) * ln)
        y = _sigmoid(og_ref[rows, hs].astype(F32)) * (_rms(hsum, ng) + skip * uc_ref[rows, hs].astype(F32))
        y_ref[rows, hs] = y.astype(BF16)

    pending = None
    for c in range(nc):
        r = c + 1
        qc = q_ref[c * ln:(c + 1) * ln, hs]
        kk, vv = chunk_kv(r)
        s = scores[c]
        qf = qc.astype(F32)
        lhs = []
        p_sum = None
        for dirn, causal, st_ref in ((fw, sj <= ti, cf_ref), (bw, sj >= ti, cb_ref)):
            big_m = lane_bcast(dirn["M"][r:r + 1])
            p = s * jnp.exp(jnp.where(causal, dirn["a"][r:r + 1] - wide(big_m), NEG))
            iw = jnp.exp(dirn["m_in"][r:r + 1] - big_m)
            den = _dot(p.astype(BF16), ones) + iw * _dot(qc, st_ref[c, :, dh:])
            rinv = 1.0 / jnp.maximum(jnp.abs(den), jnp.exp(-lane_bcast(dirn["mt"][r:r + 1])))
            p_sum = p * wide(rinv) if p_sum is None else p_sum + p * wide(rinv)
            lhs.append((qf * wide(iw * rinv)).astype(BF16))
        if pending is not None:
            finish(*pending)
        pending = (c, jnp.concatenate([p_sum.astype(BF16)] + lhs, axis=1), vv)
    finish(*pending)


def _mlstm(q, k, v, kc, vc, gl, gc, og, uc, ng, skip):
    bsz, t, wd = q.shape
    dh = MLSTM_HEAD_DIM
    nh = wd // dh
    nc = t // MLSTM_CHUNK
    n_ctx = kc.shape[1]
    assert n_ctx == MLSTM_CHUNK and gl.shape == (bsz, nh, 4, nc, MLSTM_CHUNK)
    hg = MLSTM_HEADS_PER_STEP
    assert nh % hg == 0
    seq = lambda n: pl.BlockSpec((None, n, hg * dh), lambda b, h: (b, 0, h))
    head_vec = pl.BlockSpec((1, hg * dh), lambda b, h: (0, h))
    state = pltpu.VMEM((nc, dh, dh + LANES), BF16)
    return pl.pallas_call(
        functools.partial(_mlstm_kernel, nc=nc, heads=hg),
        out_shape=jax.ShapeDtypeStruct((bsz, t, wd), BF16),
        grid=(bsz, nh // hg),
        in_specs=[seq(t), seq(t), seq(t), seq(n_ctx), seq(n_ctx),
                  pl.BlockSpec((None, hg, 4, nc, MLSTM_CHUNK), lambda b, h: (b, h, 0, 0, 0)),
                  pl.BlockSpec((None, hg, 4, 1, MLSTM_CHUNK), lambda b, h: (b, h, 0, 0, 0)),
                  seq(t), seq(t), head_vec, head_vec],
        out_specs=seq(t),
        scratch_shapes=[state] * (2 * hg),
        compiler_params=_params("parallel", "parallel"),
        name="mlstm",
    )(q, k, v, kc, vc, gl, gc, og, uc, ng, skip)


def _out1_kernel(x_ref, y_ref, wo_ref, g_ref, mod_ref, o_ref):
    y = _dot(y_ref[...], wo_ref[...])
    o_ref[...] = x_ref[...] + mod_ref[2:3, :] * _rms(y, g_ref[...])


def _out1(x, y, w_out, g, mod):
    bsz, t, d = x.shape
    tm = min(t, STREAM_ROWS)
    wd = y.shape[2]
    tile = lambda n: pl.BlockSpec((None, tm, n), lambda b, i: (b, i, 0))
    return pl.pallas_call(
        _out1_kernel,
        out_shape=jax.ShapeDtypeStruct((bsz, t, d), F32),
        grid=(bsz, t // tm),
        in_specs=[tile(d), tile(wd), _const_spec(w_out.shape), _const_spec(g.shape),
                  pl.BlockSpec((None, 6, d), lambda b, i: (b, 0, 0))],
        out_specs=tile(d),
        compiler_params=_params("parallel", "parallel"),
        name="out1",
    )(x, y, w_out, g, mod)


def _rope_tables(t_len, scale):
    rows = t_len // GRID_W
    row = jnp.repeat(jnp.arange(rows, dtype=F32), GRID_W)
    col = jnp.tile(jnp.arange(GRID_W, dtype=F32), rows)
    n_freq = HEAD_DIM // 4
    inv = ROPE_BASE ** (-jnp.arange(n_freq, dtype=F32) / n_freq)
    ar, ac = row[:, None] * inv, col[:, None] * inv
    cos = jnp.concatenate([jnp.cos(ar), jnp.cos(ar), jnp.cos(ac), jnp.cos(ac)], axis=-1)
    sin = jnp.concatenate([-jnp.sin(ar), jnp.sin(ar), -jnp.sin(ac), jnp.sin(ac)], axis=-1)
    reps = LANES // HEAD_DIM
    return jnp.tile(cos, (1, reps)) * scale, jnp.tile(sin, (1, reps)) * scale


def _ffn_weights(w_up, conv_w, conv_b, w_down):
    return w_up.astype(BF16), conv_w, conv_b.reshape(1, -1), w_down.astype(BF16)


def kernel(x, c, ctx, c_ctx, mod_w, mod_b, norm_g, attn_in_w, attn_sink, pool_w, pool_scale, attn_out_w,
           rec_in_w, rec_gate_b, rec_conv_w, rec_conv_b, rec_q_w, rec_k_w, rec_norm_g, rec_skip, rec_out_w,
           ffn_up_w, ffn_conv_w, ffn_conv_b, ffn_down_w):
    bsz, t, d = x.shape
    n_ctx = ctx.shape[1]
    ctx_row = bsz

    pad_rows = (-(bsz + 1)) % 16
    c_all = jnp.concatenate([c, c_ctx[None, :], jnp.zeros((pad_rows, d), F32)], axis=0)
    mod = _adaln(c_all, mod_w, mod_b)
    mod = mod.reshape(mod.shape[0], mod.shape[1], 6, d)
    row2 = lambda a: a.reshape(1, -1)

    w_in0 = attn_in_w[0].astype(BF16)
    qscale = HEAD_DIM ** -0.5
    tables = _rope_tables(t, qscale) + _rope_tables(t, 1.0)
    g00 = row2(norm_g[0, 0])
    q, k4, v4, u = _in0(x, mod[0], None, g00, w_in0, tables)
    qc, kc4, vc4, ucx = _in0(ctx, mod[0], ctx_row, g00, w_in0, None)
    pw = pool_w[0].astype(BF16)
    psc = row2(pool_scale[0])
    wo0 = attn_out_w[0].astype(BF16)
    g01 = row2(norm_g[0, 1])
    sink = attn_sink[0]
    x1 = _mix0(x, q, k4, v4, kc4, vc4, u, sink, pw, psc, wo0, g01, mod[0], None)
    c1 = _mix0(ctx, qc, None, None, kc4, vc4, ucx, sink, pw, psc, wo0, g01, mod[0], ctx_row)
    f0 = _ffn_weights(ffn_up_w[0], ffn_conv_w[0], ffn_conv_b[0], ffn_down_w[0])
    g02, g03 = row2(norm_g[0, 2]), row2(norm_g[0, 3])
    x2 = _ffn(x1, mod[0], None, g02, g03, *f0)
    c2 = _ffn(c1, mod[0], ctx_row, g02, g03, *f0)

    w = rec_in_w[0]
    wd = MLSTM_HEADS * MLSTM_HEAD_DIM
    ng = 4 * MLSTM_HEADS
    w_main = w[:, :3 * wd].astype(BF16)
    w_gate = jnp.concatenate([w[:, 3 * wd:], jnp.zeros((d, LANES - ng), F32)], axis=1).astype(BF16)
    gbias = jnp.concatenate([rec_gate_b[0].reshape(1, ng), jnp.zeros((1, LANES - ng), F32)], axis=1)
    cw, cb = rec_conv_w[0], row2(rec_conv_b[0])
    qw = rec_q_w[0].astype(BF16)
    kw = (rec_k_w[0] * (MLSTM_HEAD_DIM ** -0.5)).astype(BF16)
    g10 = row2(norm_g[1, 0])
    uc, og, v, qm, km, gts = _in1(x2, mod[1], None, g10, w_main, w_gate, gbias, cw, cb, qw, kw, True)
    vcx, kcx, gtc = _in1(c2, mod[1], ctx_row, g10, w_main, w_gate, gbias, cw, cb, None, kw, False)

    def gate_rows(gt, n):
        gt = gt[:, :, :ng].reshape(bsz, n // MLSTM_CHUNK, MLSTM_CHUNK, 4, MLSTM_HEADS)
        return gt.transpose(0, 4, 3, 1, 2)

    y = _mlstm(qm, km, v, kcx, vcx, gate_rows(gts, t), gate_rows(gtc, n_ctx), og, uc,
               row2(rec_norm_g[0]), row2(rec_skip[0]))
    x3 = _out1(x2, y, rec_out_w[0].astype(BF16), row2(norm_g[1, 1]), mod[1])
    f1 = _ffn_weights(ffn_up_w[1], ffn_conv_w[1], ffn_conv_b[1], ffn_down_w[1])
    return _ffn(x3, mod[1], None, row2(norm_g[1, 2]), row2(norm_g[1, 3]), *f1)
```

```python
import functools

import jax
import jax.numpy as jnp
from jax import lax
from jax.experimental import pallas as pl
from jax.experimental.pallas import tpu as pltpu

F32 = jnp.float32
BF16 = jnp.bfloat16

GRID_W = 64
HEAD_DIM = 64
ATTN_Q_HEADS = 8
ATTN_KV_HEADS = 2
ATTN_BLOCK = 128
ROPE_BASE = 10000.0
POOL_WINDOWS = (2, 4, 8, 16)
POOL_CH = 128
MLSTM_HEADS = 4
MLSTM_HEAD_DIM = 256
MLSTM_CHUNK = 256
MLSTM_HEADS_PER_STEP = 2
EPS = 1e-6
NEG = -1e30

LANES = 128
SUBLANES = 8
VMEM_LIMIT = 56 * 1024 * 1024


def _params(*sem):
    return pltpu.CompilerParams(dimension_semantics=sem, vmem_limit_bytes=VMEM_LIMIT)


def _rms(x, g):
    return x * lax.rsqrt(jnp.mean(x * x, axis=-1, keepdims=True) + EPS) * g


def _sigmoid(x):
    return 1.0 / (1.0 + jnp.exp(-x))


def _dot(a, b):
    return jnp.dot(a, b, preferred_element_type=F32)


def _dot_nt(a, b):
    return lax.dot_general(a, b, (((1,), (1,)), ((), ())), preferred_element_type=F32)


def _dot_tn(a, b):
    return lax.dot_general(a, b, (((0,), (0,)), ((), ())), preferred_element_type=F32)


def _const_spec(shape):
    nd = len(shape)
    return pl.BlockSpec(shape, lambda *_: (0,) * nd)


def _adaln_kernel(c_ref, w_ref, b_ref, o_ref):
    c = c_ref[...]
    s = c * _sigmoid(c)
    o_ref[...] = _dot(s.astype(BF16), w_ref[...].astype(BF16)) + b_ref[...]


def _adaln(c_all, mod_w, mod_b):
    depth, d, n6 = mod_w.shape
    rows = c_all.shape[0]
    tn = 1536
    return pl.pallas_call(
        _adaln_kernel,
        out_shape=jax.ShapeDtypeStruct((depth, rows, n6), F32),
        grid=(depth, n6 // tn),
        in_specs=[
            pl.BlockSpec((rows, d), lambda l, j: (0, 0)),
            pl.BlockSpec((None, d, tn), lambda l, j: (l, 0, j)),
            pl.BlockSpec((None, 1, tn), lambda l, j: (l, 0, j)),
        ],
        out_specs=pl.BlockSpec((None, rows, tn), lambda l, j: (l, 0, j)),
        compiler_params=_params("arbitrary", "arbitrary"),
        name="adaln",
    )(c_all, mod_w, mod_b.reshape(depth, 1, n6))


def _in0_kernel(*refs, rope):
    if rope:
        x_ref, mod_ref, g_ref, w_ref, cq_ref, sq_ref, ck_ref, sk_ref, q_ref, k_ref, v_ref, u_ref = refs
    else:
        x_ref, mod_ref, g_ref, w_ref, q_ref, k_ref, v_ref, u_ref = refs
    x = x_ref[...]
    h = _rms(x, g_ref[...]) * (1.0 + mod_ref[1:2, :]) + mod_ref[0:1, :]
    y = _dot(h.astype(BF16), w_ref[...])
    nq = q_ref.shape[1]
    nkv = ATTN_KV_HEADS * HEAD_DIM
    tm = x.shape[0]
    lane = lax.broadcasted_iota(jnp.int32, (tm, LANES), 1)
    first = lane < HEAD_DIM

    def dup(a):
        sw = pltpu.roll(a, HEAD_DIM, 1)
        return jnp.concatenate([jnp.where(first, a, sw), jnp.where(first, sw, a)], axis=1)

    def put_keys(kf):
        for jb in range(tm // ATTN_BLOCK):
            k_ref[jb] = jnp.transpose(kf[jb * ATTN_BLOCK:(jb + 1) * ATTN_BLOCK, :]).astype(BF16)

    if rope:
        even = (lane // 16) % 2 == 0

        def rot(a, c, s):
            sw = jnp.where(even, pltpu.roll(a, LANES - 16, 1), pltpu.roll(a, 16, 1))
            return a * c + sw * s

        cq, sq, ck, sk = cq_ref[...], sq_ref[...], ck_ref[...], sk_ref[...]
        for j in range(nq // LANES):
            q_ref[:, j * LANES:(j + 1) * LANES] = rot(y[:, j * LANES:(j + 1) * LANES], cq, sq).astype(BF16)
        put_keys(dup(rot(y[:, nq:nq + nkv], ck, sk)))
    else:
        q_ref[...] = (y[:, :nq] * (HEAD_DIM ** -0.5)).astype(BF16)
        put_keys(dup(y[:, nq:nq + nkv]))
    v_ref[...] = dup(y[:, nq + nkv:nq + 2 * nkv]).astype(BF16)
    u_ref[...] = y[:, nq + 2 * nkv:].astype(BF16)


def _in0(x, mod, mod_row, g, w, tables):
    bsz, t, d = x.shape
    tm = min(t, STREAM_ROWS)
    rope = tables is not None
    nq, nk, nu = 512, 256, 512
    row = (lambda b, i: (b, 0, 0)) if mod_row is None else (lambda b, i: (mod_row, 0, 0))
    in_specs = [
        pl.BlockSpec((None, tm, d), lambda b, i: (b, i, 0)),
        pl.BlockSpec((None, 6, d), row),
        _const_spec((1, d)),
        _const_spec(w.shape),
    ]
    args = [x, mod, g, w]
    if rope:
        in_specs += [pl.BlockSpec((tm, LANES), lambda b, i: (i, 0))] * 4
        args += list(tables)
    rows = lambda n: (jax.ShapeDtypeStruct((bsz, t, n), BF16), pl.BlockSpec((None, tm, n), lambda b, i: (b, i, 0)))
    keys_t = (jax.ShapeDtypeStruct((bsz, t // ATTN_BLOCK, nk, ATTN_BLOCK), BF16),
              pl.BlockSpec((None, tm // ATTN_BLOCK, nk, ATTN_BLOCK), lambda b, i: (b, i, 0, 0)))
    outs = [rows(nq), keys_t, rows(nk), rows(nu)]
    return pl.pallas_call(
        functools.partial(_in0_kernel, rope=rope),
        out_shape=[o[0] for o in outs],
        grid=(bsz, t // tm),
        in_specs=in_specs,
        out_specs=[o[1] for o in outs],
        compiler_params=_params("parallel", "parallel"),
        name="in0_rope" if rope else "in0_ctx",
    )(*args)


def _mix0_kernel(*refs, band, t_len, tq):
    if band:
        (sink_ref, x_ref, q_ref, k_ref, v_ref, kc_ref, vc_ref, u_ref, pw_ref, ps_ref, wo_ref, g_ref, mod_ref,
         o_ref, cat_ref) = refs
    else:
        (sink_ref, x_ref, q_ref, kc_ref, vc_ref, u_ref, pw_ref, ps_ref, wo_ref, g_ref, mod_ref,
         o_ref, cat_ref) = refs
    blk = ATTN_BLOCK
    nb = t_len // blk
    nsub = tq // blk
    tstep = pl.program_id(1)
    n_ctx = vc_ref.shape[0]
    n_keys = (3 * blk if band else 0) + n_ctx

    qrow = jnp.bitwise_and(lax.broadcasted_iota(jnp.int32, (2 * blk, blk), 0), blk - 1)
    kcol = lax.broadcasted_iota(jnp.int32, (2 * blk, blk), 1)
    tri_prev = kcol >= qrow
    tri_next = kcol <= qrow
    top_rows = lax.broadcasted_iota(jnp.int32, (2 * blk, 1), 0) < blk
    dim_lo = lax.broadcasted_iota(jnp.int32, (LANES, n_keys), 0) < HEAD_DIM
    lane_lo = lax.broadcasted_iota(jnp.int32, (n_keys, LANES), 1) < HEAD_DIM
    lane_o = lax.broadcasted_iota(jnp.int32, (2 * blk, LANES), 1) < HEAD_DIM
    zero = jnp.zeros((), BF16)
    ones_lo = jnp.where(lane_lo, 1.0, 0.0).astype(BF16)
    ones_hi = jnp.where(lane_lo, 0.0, 1.0).astype(BF16)
    row3 = lax.broadcasted_iota(jnp.int32, (blk, 3 * blk), 0)
    col3 = lax.broadcasted_iota(jnp.int32, (blk, 3 * blk), 1) - blk
    pool_band = [((col3 >= row3 - w // 2) & (col3 <= row3 + w // 2 - 1)).astype(BF16) for w in POOL_WINDOWS]
    row_pos = lax.broadcasted_iota(jnp.int32, (blk, LANES), 0)

    for j in range(nsub):
        n = tstep * nsub + j
        pblk = jnp.maximum(n - 1, 0)
        nblk = jnp.minimum(n + 1, nb - 1)
        r0 = pl.multiple_of(n * blk, blk)
        ps = pl.multiple_of(pblk * blk, blk)
        ns = pl.multiple_of(nblk * blk, blk)
        has_prev = n > 0
        has_next = n < nb - 1
        rows = slice(j * blk, (j + 1) * blk)

        if band:
            vb = jnp.concatenate([v_ref[pl.ds(ps, blk), :], v_ref[pl.ds(r0, blk), :], v_ref[pl.ds(ns, blk), :],
                                  vc_ref[...]], axis=0)
        else:
            vb = vc_ref[...]
        for hk in range(ATTN_KV_HEADS):
            hd = slice(hk * LANES, (hk + 1) * LANES)
            kt = [kc_ref[i, hd, :] for i in range(kc_ref.shape[0])]
            if band:
                kt = [k_ref[pblk, hd, :], k_ref[n, hd, :], k_ref[nblk, hd, :]] + kt
            kt = jnp.concatenate(kt, axis=1)
            vv = vb[:, hd]
            v_aug = jnp.concatenate(
                [jnp.concatenate([jnp.where(lane_lo, vv, zero), ones_lo], axis=1),
                 jnp.concatenate([jnp.where(lane_lo, zero, vv), ones_hi], axis=1)], axis=0)
            c0, c1 = 2 * hk, 2 * hk + 1
            q2 = jnp.concatenate([q_ref[rows, c0 * LANES:(c0 + 1) * LANES],
                                  q_ref[rows, c1 * LANES:(c1 + 1) * LANES]], axis=0)
            probs, esink = [], []
            s_both = _dot(q2, jnp.concatenate([jnp.where(dim_lo, kt, zero), jnp.where(dim_lo, zero, kt)], axis=1))
            for half in range(2):
                s = s_both[:, half * n_keys:(half + 1) * n_keys]
                if band:
                    s = jnp.concatenate([jnp.where(tri_prev & has_prev, s[:, :blk], NEG), s[:, blk:2 * blk],
                                         jnp.where(tri_next & has_next, s[:, 2 * blk:3 * blk], NEG),
                                         s[:, 3 * blk:]], axis=1)
                snk = jnp.where(top_rows, sink_ref[4 * hk + half], sink_ref[4 * hk + 2 + half])
                m = jnp.maximum(jnp.max(s, axis=-1, keepdims=True), snk)
                probs.append(jnp.exp(s - m).astype(BF16))
                esink.append(jnp.exp(snk - m))
            o2 = _dot(jnp.concatenate(probs, axis=1), v_aug)
            o = o2[:, :LANES] / (o2[:, LANES:] + jnp.where(lane_o, esink[0], esink[1]))
            cat_ref[rows, c0 * LANES:(c0 + 1) * LANES] = o[:blk].astype(BF16)
            cat_ref[rows, c1 * LANES:(c1 + 1) * LANES] = o[blk:].astype(BF16)

        uo = u_ref[pl.ds(r0, blk), :]
        up = jnp.where(has_prev, u_ref[pl.ds(ps, blk), :], zero)
        un = jnp.where(has_next, u_ref[pl.ds(ns, blk), :], zero)
        ub = jnp.concatenate([up, uo, un], axis=0)
        pos = r0 + row_pos
        for gi, w in enumerate(POOL_WINDOWS):
            cs = slice(gi * POOL_CH, (gi + 1) * POOL_CH)
            sums = _dot(pool_band[gi], ub[:, cs])
            cnt = jnp.minimum(pos + w // 2, t_len) - jnp.maximum(pos - w // 2, 0)
            dlt = sums / cnt.astype(F32) - uo[:, cs].astype(F32)
            y = _dot(dlt.astype(BF16), pw_ref[gi]) * ps_ref[:, cs]
            off = ATTN_Q_HEADS * HEAD_DIM + gi * POOL_CH
            cat_ref[rows, off:off + POOL_CH] = y.astype(BF16)

    y = _dot(cat_ref[...], wo_ref[...])
    o_ref[...] = x_ref[...] + mod_ref[2:3, :] * _rms(y, g_ref[...])


def _mix0(x, q, k4, v4, kc4, vc4, u, sink, pool_w, pool_scale, w_out, g, mod, mod_row):
    bsz, t, d = x.shape
    band = k4 is not None
    tq = min(t, LIGHT_ROWS)
    n_ctx = vc4.shape[1]
    row = (lambda b, i: (b, 0, 0)) if mod_row is None else (lambda b, i: (mod_row, 0, 0))
    full = lambda n, w: pl.BlockSpec((None, n, w), lambda b, i: (b, 0, 0))
    keys_t = lambda a: pl.BlockSpec((None,) + a.shape[1:], lambda b, i: (b, 0, 0, 0))
    in_specs = [
        pl.BlockSpec(memory_space=pltpu.SMEM),
        pl.BlockSpec((None, tq, d), lambda b, i: (b, i, 0)),
        pl.BlockSpec((None, tq, q.shape[2]), lambda b, i: (b, i, 0)),
    ]
    args = [sink, x, q]
    if band:
        in_specs += [keys_t(k4), full(t, v4.shape[2])]
        args += [k4, v4]
    in_specs += [keys_t(kc4), full(n_ctx, vc4.shape[2]), full(t, u.shape[2]),
                 _const_spec(pool_w.shape), _const_spec(pool_scale.shape), _const_spec(w_out.shape),
                 _const_spec(g.shape), pl.BlockSpec((None, 6, d), row)]
    args += [kc4, vc4, u, pool_w, pool_scale, w_out, g, mod]
    return pl.pallas_call(
        functools.partial(_mix0_kernel, band=band, t_len=t, tq=tq),
        out_shape=jax.ShapeDtypeStruct((bsz, t, d), F32),
        grid=(bsz, t // tq),
        in_specs=in_specs,
        out_specs=pl.BlockSpec((None, tq, d), lambda b, i: (b, i, 0)),
        scratch_shapes=[pltpu.VMEM((tq, w_out.shape[0]), BF16)],
        compiler_params=_params("parallel", "parallel"),
        name="mix0_band" if band else "mix0_ctx",
    )(*args)


HALO = SUBLANES
FFN_CHUNK = 256
FFN_ROWS = 512
LIGHT_ROWS = 1024
STREAM_ROWS = 2048


def _halo_specs(tm, t, d):
    nblk = tm // HALO
    last = t // HALO - 1
    return [
        pl.BlockSpec((None, tm, d), lambda b, i: (b, i, 0)),
        pl.BlockSpec((None, HALO, d), lambda b, i: (b, jnp.maximum(i * nblk - 1, 0), 0)),
        pl.BlockSpec((None, HALO, d), lambda b, i: (b, jnp.minimum((i + 1) * nblk, last), 0)),
    ]


def _conv3(u, cw, cb, tm):
    return (cw[0:1, :] * u[HALO - 1:HALO - 1 + tm, :] + cw[1:2, :] * u[HALO:HALO + tm, :]
            + cw[2:3, :] * u[HALO + 1:HALO + 1 + tm, :] + cb)


def _ffn_kernel(xm_ref, xp_ref, xn_ref, mod_ref, gpre_ref, gpost_ref, wup_ref, cw_ref, cb_ref, wdn_ref,
                o_ref, hb_ref, acc_ref, u0_ref, u1_ref, *, tm, nt):
    i = pl.program_id(1)
    shift, scale, gate = mod_ref[3:4, :], mod_ref[4:5, :], mod_ref[5:6, :]
    gpre = gpre_ref[...]

    def pre(xv):
        return _rms(xv, gpre) * (1.0 + scale) + shift

    hp = jnp.where(i > 0, pre(xp_ref[...]), 0.0)
    hn = jnp.where(i < nt - 1, pre(xn_ref[...]), 0.0)
    hb_ref[...] = jnp.concatenate([hp, pre(xm_ref[...]), hn], axis=0).astype(BF16)
    acc_ref[...] = jnp.zeros_like(acc_ref)
    dff = wdn_ref.shape[0]
    nchunk = dff // FFN_CHUNK
    half = FFN_CHUNK // LANES
    u_refs = (u0_ref, u1_ref)

    def cols(c, j):
        o = (dff if j >= half else 0) + c * FFN_CHUNK + (j % half) * LANES
        return slice(o, o + LANES)

    def up(c, s):
        for part in range(2):
            o = part * dff + c * FFN_CHUNK
            u = _dot(hb_ref[...], wup_ref[:, o:o + FFN_CHUNK])
            for j in range(half):
                u_refs[s][part * half + j] = u[:, j * LANES:(j + 1) * LANES]

    def conv_act(c, s):
        ur = u_refs[s]

        def conv(j):
            cs = cols(c, j)
            return (cw_ref[0:1, cs] * ur[j, pl.ds(HALO - 1, tm), :] + cw_ref[1:2, cs] * ur[j, pl.ds(HALO, tm), :]
                    + cw_ref[2:3, cs] * ur[j, pl.ds(HALO + 1, tm), :] + cb_ref[:, cs])

        acts = []
        for j in range(half):
            ag, av = conv(j), conv(half + j)
            acts.append((ag * _sigmoid(ag) * av).astype(BF16))
        return jnp.concatenate(acts, axis=1)

    def down(c, act):
        acc_ref[...] += _dot(act, wdn_ref[c * FFN_CHUNK:(c + 1) * FFN_CHUNK, :])

    up(0, 0)
    prev = None
    for c in range(nchunk):
        if c + 1 < nchunk:
            up(c + 1, (c + 1) % 2)
        if prev is not None:
            down(c - 1, prev)
        prev = conv_act(c, c % 2)
    down(nchunk - 1, prev)
    o_ref[...] = xm_ref[...] + gate * _rms(acc_ref[...], gpost_ref[...])


def _ffn(x, mod, mod_row, gpre, gpost, wup, cw, cb, wdn):
    bsz, t, d = x.shape
    tm = min(t, FFN_ROWS)
    nt = t // tm
    assert wdn.shape[0] % FFN_CHUNK == 0
    row = (lambda b, i: (b, 0, 0)) if mod_row is None else (lambda b, i: (mod_row, 0, 0))
    in_specs = _halo_specs(tm, t, d) + [
        pl.BlockSpec((None, 6, d), row), _const_spec(gpre.shape), _const_spec(gpost.shape),
        _const_spec(wup.shape), _const_spec(cw.shape), _const_spec(cb.shape), _const_spec(wdn.shape)]
    u_buf = pltpu.VMEM((2 * FFN_CHUNK // LANES, tm + 2 * HALO, LANES), F32)
    return pl.pallas_call(
        functools.partial(_ffn_kernel, tm=tm, nt=nt),
        out_shape=jax.ShapeDtypeStruct((bsz, t, d), F32),
        grid=(bsz, nt),
        in_specs=in_specs,
        out_specs=pl.BlockSpec((None, tm, d), lambda b, i: (b, i, 0)),
        scratch_shapes=[pltpu.VMEM((tm + 2 * HALO, d), BF16), pltpu.VMEM((tm, d), F32), u_buf, u_buf],
        compiler_params=_params("parallel", "parallel"),
        name="ffn",
    )(x, x, x, mod, gpre, gpost, wup, cw, cb, wdn)


def _in1_kernel(*refs, tm, nt, full):
    if full:
        (xm_ref, xp_ref, xn_ref, mod_ref, g_ref, w_ref, wg_ref, gb_ref, cw_ref, cb_ref, qw_ref, kw_ref,
         uc_ref, og_ref, v_ref, q_ref, k_ref, gt_ref) = refs
    else:
        (xm_ref, xp_ref, xn_ref, mod_ref, g_ref, w_ref, wg_ref, gb_ref, cw_ref, cb_ref, kw_ref,
         v_ref, k_ref, gt_ref) = refs
    i = pl.program_id(1)
    shift, scale = mod_ref[0:1, :], mod_ref[1:2, :]
    g = g_ref[...]

    def pre(xv):
        return _rms(xv, g) * (1.0 + scale) + shift

    hm = pre(xm_ref[...])
    hp = jnp.where(i > 0, pre(xp_ref[...]), 0.0)
    hn = jnp.where(i < nt - 1, pre(xn_ref[...]), 0.0)
    hext = jnp.concatenate([hp, hm, hn], axis=0).astype(BF16)
    dv = v_ref.shape[1]
    u = _dot(hext, w_ref[:, :dv])
    uc = _conv3(u, cw_ref[...], cb_ref[...], tm)
    uc = uc * _sigmoid(uc)
    ucb = uc.astype(BF16)
    hmb = hm.astype(BF16)
    v_ref[...] = _dot(hmb, w_ref[:, dv:2 * dv]).astype(BF16)
    gt_ref[...] = _dot(hmb, wg_ref[...]) + gb_ref[...]
    if full:
        og_ref[...] = _dot(hmb, w_ref[:, 2 * dv:3 * dv]).astype(BF16)
        uc_ref[...] = ucb
    dh = MLSTM_HEAD_DIM
    for hh in range(MLSTM_HEADS):
        uh = ucb[:, hh * dh:(hh + 1) * dh]
        k_ref[:, hh * dh:(hh + 1) * dh] = _dot(uh, kw_ref[hh]).astype(BF16)
        if full:
            q_ref[:, hh * dh:(hh + 1) * dh] = _dot(uh, qw_ref[hh]).astype(BF16)


def _in1(x, mod, mod_row, g, w, wg, gb, cw, cb, qw, kw, full):
    bsz, t, d = x.shape
    tm = min(t, LIGHT_ROWS)
    nt = t // tm
    dw = w.shape[1] // 3
    row = (lambda b, i: (b, 0, 0)) if mod_row is None else (lambda b, i: (mod_row, 0, 0))
    in_specs = _halo_specs(tm, t, d) + [pl.BlockSpec((None, 6, d), row)]
    consts = [g, w, wg, gb, cw, cb] + ([qw] if full else []) + [kw]
    in_specs += [_const_spec(a.shape) for a in consts]
    act = lambda: jax.ShapeDtypeStruct((bsz, t, dw), BF16)
    gates = jax.ShapeDtypeStruct((bsz, t, LANES), F32)
    out_shape = [act(), act(), act(), act(), act(), gates] if full else [act(), act(), gates]
    out_specs = [pl.BlockSpec((None, tm, s.shape[2]), lambda b, i: (b, i, 0)) for s in out_shape]
    return pl.pallas_call(
        functools.partial(_in1_kernel, tm=tm, nt=nt, full=full),
        out_shape=out_shape,
        grid=(bsz, nt),
        in_specs=in_specs,
        out_specs=out_specs,
        compiler_params=_params("parallel", "parallel"),
        name="in1_full" if full else "in1_ctx",
    )(x, x, x, mod, *consts)


def _log_sigmoid(x):
    return jnp.minimum(x, 0.0) - jnp.log1p(jnp.exp(-jnp.abs(x)))


def _scan_lanes(x, op, fill, reverse):
    n = x.shape[1]
    lane = lax.broadcasted_iota(jnp.int32, x.shape, 1)
    d = 1
    while d < n:
        if reverse:
            shifted = jnp.where(lane < n - d, pltpu.roll(x, n - d, 1), fill)
        else:
            shifted = jnp.where(lane >= d, pltpu.roll(x, d, 1), fill)
        x = op(x, shifted)
        d *= 2
    return x


def _mlstm_kernel(*refs, nc, heads):
    io, states = refs[:12], refs[12:]
    for hh in range(heads):
        _mlstm_head(hh, *io, states[2 * hh], states[2 * hh + 1], nc=nc)


def _mlstm_head(hh, q_ref, k_ref, v_ref, kc_ref, vc_ref, gl_ref, gc_ref, og_ref, uc_ref, ng_ref, skip_ref,
                y_ref, cf_ref, cb_ref, *, nc):
    ln = MLSTM_CHUNK
    dh = MLSTM_HEAD_DIM
    hs = slice(hh * dh, (hh + 1) * dh)
    nr = nc + 1
    scores = [_dot_nt(q_ref[c * ln:(c + 1) * ln, hs], k_ref[c * ln:(c + 1) * ln, hs]) for c in range(nc)]
    gates = [jnp.concatenate([gc_ref[hh, gi], gl_ref[hh, gi]], axis=0) for gi in range(4)]

    def direction(ig, fpre, reverse):
        b = _scan_lanes(_log_sigmoid(fpre), jnp.add, 0.0, reverse)
        a = ig - b
        cmax = _scan_lanes(a, jnp.maximum, NEG, reverse)
        e = 0 if reverse else ln - 1
        b_end, a_max = b[:, e:e + 1], cmax[:, e:e + 1]
        order = [0] + (list(range(nc, 0, -1)) if reverse else list(range(1, nr)))
        m_in = [None] * nr
        m = jnp.zeros((1, 1), F32)
        for r in order:
            m_in[r] = m
            m = b_end[r:r + 1] + jnp.maximum(m, a_max[r:r + 1])
        m_in = jnp.concatenate(m_in, axis=0)
        big_m = jnp.maximum(m_in, cmax)
        m_end = jnp.maximum(m_in, a_max)
        l2e = 1.4426950408889634
        return dict(a=a * l2e, M=big_m * l2e, mt=(b + big_m) * l2e, m_in=m_in * l2e, wts=jnp.exp(a - m_end),
                    decay=jnp.exp(m_in - m_end))

    fw = direction(gates[0], gates[1], False)
    bw = direction(gates[2], gates[3], True)

    def lane_bcast(row):
        return jnp.transpose(jnp.broadcast_to(row, (LANES, ln)))

    def wide(a):
        return jnp.concatenate([a] * (dh // LANES), axis=1)

    ones = jnp.ones((ln, LANES), BF16)

    def chunk_kv(r):
        if r == 0:
            return kc_ref[:, hs], vc_ref[:, hs]
        return k_ref[(r - 1) * ln:r * ln, hs], v_ref[(r - 1) * ln:r * ln, hs]

    contrib_f, contrib_b = [], []
    for r in range(nr):
        kk, vv = chunk_kv(r)
        kt = jnp.transpose(kk.astype(F32))
        v_aug = jnp.concatenate([vv, ones], axis=1)
        contrib_f.append(_dot((kt * fw["wts"][r:r + 1]).astype(BF16), v_aug))
        contrib_b.append(_dot((kt * bw["wts"][r:r + 1]).astype(BF16), v_aug))

    st = contrib_f[0]
    for c in range(nc):
        cf_ref[c] = st.astype(BF16)
        if c + 1 < nc:
            st = fw["decay"][c + 1:c + 2] * st + contrib_f[c + 1]
    st = contrib_b[0]
    for c in range(nc - 1, -1, -1):
        cb_ref[c] = st.astype(BF16)
        if c > 0:
            st = bw["decay"][c + 1:c + 2] * st + contrib_b[c + 1]

    ti = lax.broadcasted_iota(jnp.int32, (ln, ln), 0)
    sj = lax.broadcasted_iota(jnp.int32, (ln, ln), 1)
    ng = ng_ref[:, hs]
    skip = skip_ref[:, hs]
    def finish(c, lhs, vv):
        rhs = jnp.concatenate([vv, cf_ref[c, :, :dh], cb_ref[c, :, :dh]], axis=0)
        hsum = _dot(lhs, rhs)
        rows = slice(c * ln, (c + 1) * ln)
        y = _sigmoid(og_ref[rows, hs].astype(F32)) * (_rms(hsum, ng) + skip * uc_ref[rows, hs].astype(F32))
        y_ref[rows, hs] = y.astype(BF16)

    pending = None
    for c in range(nc):
        r = c + 1
        qc = q_ref[c * ln:(c + 1) * ln, hs]
        kk, vv = chunk_kv(r)
        s = scores[c]
        qf = qc.astype(F32)
        lhs = []
        p_sum = None
        for dirn, causal, st_ref in ((fw, sj <= ti, cf_ref), (bw, sj >= ti, cb_ref)):
            big_m = lane_bcast(dirn["M"][r:r + 1])
            p = s * jnp.exp2(jnp.where(causal, dirn["a"][r:r + 1] - wide(big_m), NEG))
            iw = jnp.exp2(dirn["m_in"][r:r + 1] - big_m)
            den = _dot(p.astype(BF16), ones) + iw * _dot(qc, st_ref[c, :, dh:])
            rinv = 1.0 / jnp.maximum(jnp.abs(den), jnp.exp2(-lane_bcast(dirn["mt"][r:r + 1])))
            p_sum = p * wide(rinv) if p_sum is None else p_sum + p * wide(rinv)
            lhs.append((qf * wide(iw * rinv)).astype(BF16))
        if pending is not None:
            finish(*pending)
        pending = (c, jnp.concatenate([p_sum.astype(BF16)] + lhs, axis=1), vv)
    finish(*pending)


def _mlstm(q, k, v, kc, vc, gl, gc, og, uc, ng, skip):
    bsz, t, wd = q.shape
    dh = MLSTM_HEAD_DIM
    nh = wd // dh
    nc = t // MLSTM_CHUNK
    n_ctx = kc.shape[1]
    assert n_ctx == MLSTM_CHUNK and gl.shape == (bsz, nh, 4, nc, MLSTM_CHUNK)
    hg = MLSTM_HEADS_PER_STEP
    assert nh % hg == 0
    seq = lambda n: pl.BlockSpec((None, n, hg * dh), lambda b, h: (b, 0, h))
    head_vec = pl.BlockSpec((1, hg * dh), lambda b, h: (0, h))
    state = pltpu.VMEM((nc, dh, dh + LANES), BF16)
    return pl.pallas_call(
        functools.partial(_mlstm_kernel, nc=nc, heads=hg),
        out_shape=jax.ShapeDtypeStruct((bsz, t, wd), BF16),
        grid=(bsz, nh // hg),
        in_specs=[seq(t), seq(t), seq(t), seq(n_ctx), seq(n_ctx),
                  pl.BlockSpec((None, hg, 4, nc, MLSTM_CHUNK), lambda b, h: (b, h, 0, 0, 0)),
                  pl.BlockSpec((None, hg, 4, 1, MLSTM_CHUNK), lambda b, h: (b, h, 0, 0, 0)),
                  seq(t), seq(t), head_vec, head_vec],
        out_specs=seq(t),
        scratch_shapes=[state] * (2 * hg),
        compiler_params=_params("parallel", "parallel"),
        name="mlstm",
    )(q, k, v, kc, vc, gl, gc, og, uc, ng, skip)


def _out1_kernel(x_ref, y_ref, wo_ref, g_ref, mod_ref, o_ref):
    y = _dot(y_ref[...], wo_ref[...])
    o_ref[...] = x_ref[...] + mod_ref[2:3, :] * _rms(y, g_ref[...])


def _out1(x, y, w_out, g, mod):
    bsz, t, d = x.shape
    tm = min(t, STREAM_ROWS)
    wd = y.shape[2]
    tile = lambda n: pl.BlockSpec((None, tm, n), lambda b, i: (b, i, 0))
    return pl.pallas_call(
        _out1_kernel,
        out_shape=jax.ShapeDtypeStruct((bsz, t, d), F32),
        grid=(bsz, t // tm),
        in_specs=[tile(d), tile(wd), _const_spec(w_out.shape), _const_spec(g.shape),
                  pl.BlockSpec((None, 6, d), lambda b, i: (b, 0, 0))],
        out_specs=tile(d),
        compiler_params=_params("parallel", "parallel"),
        name="out1",
    )(x, y, w_out, g, mod)


def _rope_tables(t_len, scale):
    rows = t_len // GRID_W
    row = jnp.repeat(jnp.arange(rows, dtype=F32), GRID_W)
    col = jnp.tile(jnp.arange(GRID_W, dtype=F32), rows)
    n_freq = HEAD_DIM // 4
    inv = ROPE_BASE ** (-jnp.arange(n_freq, dtype=F32) / n_freq)
    ar, ac = row[:, None] * inv, col[:, None] * inv
    cos = jnp.concatenate([jnp.cos(ar), jnp.cos(ar), jnp.cos(ac), jnp.cos(ac)], axis=-1)
    sin = jnp.concatenate([-jnp.sin(ar), jnp.sin(ar), -jnp.sin(ac), jnp.sin(ac)], axis=-1)
    reps = LANES // HEAD_DIM
    return jnp.tile(cos, (1, reps)) * scale, jnp.tile(sin, (1, reps)) * scale


def _ffn_weights(w_up, conv_w, conv_b, w_down):
    return w_up.astype(BF16), conv_w, conv_b.reshape(1, -1), w_down.astype(BF16)


def kernel(x, c, ctx, c_ctx, mod_w, mod_b, norm_g, attn_in_w, attn_sink, pool_w, pool_scale, attn_out_w,
           rec_in_w, rec_gate_b, rec_conv_w, rec_conv_b, rec_q_w, rec_k_w, rec_norm_g, rec_skip, rec_out_w,
           ffn_up_w, ffn_conv_w, ffn_conv_b, ffn_down_w):
    bsz, t, d = x.shape
    n_ctx = ctx.shape[1]
    ctx_row = bsz

    pad_rows = (-(bsz + 1)) % 16
    c_all = jnp.concatenate([c, c_ctx[None, :], jnp.zeros((pad_rows, d), F32)], axis=0)
    mod = _adaln(c_all, mod_w, mod_b)
    mod = mod.reshape(mod.shape[0], mod.shape[1], 6, d)
    row2 = lambda a: a.reshape(1, -1)

    w_in0 = attn_in_w[0].astype(BF16)
    qscale = HEAD_DIM ** -0.5
    tables = _rope_tables(t, qscale) + _rope_tables(t, 1.0)
    g00 = row2(norm_g[0, 0])
    q, k4, v4, u = _in0(x, mod[0], None, g00, w_in0, tables)
    qc, kc4, vc4, ucx = _in0(ctx, mod[0], ctx_row, g00, w_in0, None)
    pw = pool_w[0].astype(BF16)
    psc = row2(pool_scale[0])
    wo0 = attn_out_w[0].astype(BF16)
    g01 = row2(norm_g[0, 1])
    sink = attn_sink[0]
    x1 = _mix0(x, q, k4, v4, kc4, vc4, u, sink, pw, psc, wo0, g01, mod[0], None)
    c1 = _mix0(ctx, qc, None, None, kc4, vc4, ucx, sink, pw, psc, wo0, g01, mod[0], ctx_row)
    f0 = _ffn_weights(ffn_up_w[0], ffn_conv_w[0], ffn_conv_b[0], ffn_down_w[0])
    g02, g03 = row2(norm_g[0, 2]), row2(norm_g[0, 3])
    x2 = _ffn(x1, mod[0], None, g02, g03, *f0)
    c2 = _ffn(c1, mod[0], ctx_row, g02, g03, *f0)

    w = rec_in_w[0]
    wd = MLSTM_HEADS * MLSTM_HEAD_DIM
    ng = 4 * MLSTM_HEADS
    w_main = w[:, :3 * wd].astype(BF16)
    w_gate = jnp.concatenate([w[:, 3 * wd:], jnp.zeros((d, LANES - ng), F32)], axis=1).astype(BF16)
    gbias = jnp.concatenate([rec_gate_b[0].reshape(1, ng), jnp.zeros((1, LANES - ng), F32)], axis=1)
    cw, cb = rec_conv_w[0], row2(rec_conv_b[0])
    qw = rec_q_w[0].astype(BF16)
    kw = (rec_k_w[0] * (MLSTM_HEAD_DIM ** -0.5)).astype(BF16)
    g10 = row2(norm_g[1, 0])
    uc, og, v, qm, km, gts = _in1(x2, mod[1], None, g10, w_main, w_gate, gbias, cw, cb, qw, kw, True)
    vcx, kcx, gtc = _in1(c2, mod[1], ctx_row, g10, w_main, w_gate, gbias, cw, cb, None, kw, False)

    def gate_rows(gt, n):
        gt = gt[:, :, :ng].reshape(bsz, n // MLSTM_CHUNK, MLSTM_CHUNK, 4, MLSTM_HEADS)
        return gt.transpose(0, 4, 3, 1, 2)

    y = _mlstm(qm, km, v, kcx, vcx, gate_rows(gts, t), gate_rows(gtc, n_ctx), og, uc,
               row2(rec_norm_g[0]), row2(rec_skip[0]))
    x3 = _out1(x2, y, rec_out_w[0].astype(BF16), row2(norm_g[1, 1]), mod[1])
    f1 = _ffn_weights(ffn_up_w[1], ffn_conv_w[1], ffn_conv_b[1], ffn_down_w[1])
    return _ffn(x3, mod[1], None, row2(norm_g[1, 2]), row2(norm_g[1, 3]), *f1)
```
